```python
import math
import jax
import jax.numpy as jnp
from jax import lax
import numpy as np

D_MODEL = 1024
BATCH = 2
SEQ = 8192
DEPTH = 4
DEC_BATCH = 128
DEC_SEQ = 4
PAST_LEN = 2048
PAGE_SIZE = 128

D_PLE = 256
D_RNN = D_MODEL
N_LRU_BLOCKS = 8
LRU_BLOCK = D_RNN // N_LRU_BLOCKS
CONV_W = 4
LRU_C = 8.0
N_HEADS = 8
N_KV_HEADS = 4
GROUP = N_HEADS // N_KV_HEADS
D_HEAD = D_MODEL // (2 * N_HEADS)
D_VHEAD = 2 * D_HEAD
D_Q = N_HEADS * 2 * D_HEAD
D_K = N_KV_HEADS * 2 * D_HEAD
D_V = N_KV_HEADS * D_VHEAD
D_ATT = N_HEADS * D_VHEAD
D_FF = -(-8 * D_MODEL // (3 * 256)) * 256
D_IN = 2 * D_RNN + D_Q + D_K + D_V + 2 * D_MODEL
SPLIT_IDX = (D_RNN, 2 * D_RNN, 2 * D_RNN + D_Q, 2 * D_RNN + D_Q + D_K, 2 * D_RNN + D_Q + D_K + D_V)
Q_BLOCK = 128
EPS = 1e-6

kernel_name = "hybrid_rglru_diffattn_decode_step"


def rmsnorm(x, g):
    xf = x.astype(jnp.float32)
    y = xf * lax.rsqrt(jnp.mean(xf * xf, axis=-1, keepdims=True) + EPS)
    return (y * g.astype(jnp.float32)).astype(x.dtype)


def alibi_slopes():
    s = 2.0 ** (-8.0 * np.arange(1, N_HEADS + 1) / N_HEADS)
    return jnp.asarray(s, dtype=jnp.float32).reshape(N_KV_HEADS, GROUP)


def causal_conv(u, buf, w, b):
    T = u.shape[1]
    up = jnp.concatenate([buf, u], axis=1)
    out = up[:, 0:T] * w[0]
    for j in range(1, CONV_W):
        out = out + up[:, j:j + T] * w[j]
    return out + b, up[:, -(CONV_W - 1):]


def block_diag(x, w, b):
    B, T, _ = x.shape
    xb = x.reshape(B, T, N_LRU_BLOCKS, LRU_BLOCK)
    return jnp.einsum('btnc,ncd->btnd', xb, w).reshape(B, T, D_RNN) + b


def rglru(x, h_prev, w_a, b_a, w_i, b_i, lam):
    r = jax.nn.sigmoid(block_diag(x, w_a, b_a)).astype(jnp.float32)
    i = jax.nn.sigmoid(block_diag(x, w_i, b_i))
    log_a = -LRU_C * r * jax.nn.softplus(-lam.astype(jnp.float32))
    a = jnp.exp(log_a)
    xin = jnp.sqrt(1.0 - jnp.exp(2.0 * log_a)) * (i * x).astype(jnp.float32)

    def step(h, ax):
        a_t, x_t = ax
        h = a_t * h + x_t
        return h, h

    h_last, hs = lax.scan(step, h_prev.astype(jnp.float32), (a.swapaxes(0, 1), xin.swapaxes(0, 1)))
    return hs.swapaxes(0, 1).astype(x.dtype), h_last.astype(h_prev.dtype)


def diff_attend(q, k, v, q_pos, k_pos, lam, slopes):
    s = jnp.einsum('bqhgcd,bshcd->bhgcqs', q, k).astype(jnp.float32) * (D_HEAD ** -0.5)
    dist = (q_pos[:, None] - k_pos[None, :]).astype(jnp.float32)
    bias = -slopes[:, :, None, None, None] * dist
    s = jnp.where(dist >= 0, s + bias, -jnp.inf)
    p = jax.nn.softmax(s, axis=-1)
    w = (p[:, :, :, 0] - lam * p[:, :, :, 1]).astype(v.dtype)
    return jnp.einsum('bhgqs,bshe->bqhge', w, v)


def attend_prompt(q, k, v, lam, slopes):
    B, T = q.shape[0], q.shape[1]
    nb = T // Q_BLOCK
    qb = q.reshape((B, nb, Q_BLOCK) + q.shape[2:]).swapaxes(0, 1)
    k_pos = jnp.arange(T)

    def one(args):
        bi, qi = args
        q_pos = bi * Q_BLOCK + jnp.arange(Q_BLOCK)
        return diff_attend(qi, k, v, q_pos, k_pos, lam, slopes)

    out = lax.map(one, (jnp.arange(nb), qb))
    return out.swapaxes(0, 1).reshape(B, T, N_KV_HEADS, GROUP, D_VHEAD)


def attend_sample(q, k_new, v_new, k_past, v_past, lam, slopes):
    k = jnp.concatenate([k_past, k_new], axis=1)
    v = jnp.concatenate([v_past, v_new], axis=1)
    S, T = k.shape[1], q.shape[1]
    q_pos = (S - T) + jnp.arange(T)
    return diff_attend(q, k, v, q_pos, jnp.arange(S), lam, slopes)


def decoder_layer(h, p_l, conv_buf, h_prev, attend, lw):
    B, T, _ = h.shape
    n = rmsnorm(h, lw['g_mix'])
    z = n @ lw['w_in']
    u, gl, q, k, v, gate_logits = jnp.split(z, SPLIT_IDX, axis=-1)
    uc, new_buf = causal_conv(u, conv_buf, lw['conv_w'], lw['conv_b'])
    y_lru, h_last = rglru(uc, h_prev, lw['w_a'], lw['b_a'], lw['w_i'], lw['b_i'], lw['lru_lambda'])
    b_lru = (y_lru * jax.nn.gelu(gl)) @ lw['w_br_lru']
    lam_init = lw['lam_init']
    lam = (jnp.exp(jnp.sum(lw['lam_q1'].astype(jnp.float32) * lw['lam_k1'].astype(jnp.float32)))
           - jnp.exp(jnp.sum(lw['lam_q2'].astype(jnp.float32) * lw['lam_k2'].astype(jnp.float32)))
           + lam_init)
    k_rows = k.reshape(B, T, N_KV_HEADS, 2 * D_HEAD)
    v_rows = v.reshape(B, T, N_KV_HEADS, D_VHEAD)
    o = attend(q.reshape(B, T, N_KV_HEADS, GROUP, 2, D_HEAD),
               k_rows.reshape(B, T, N_KV_HEADS, 2, D_HEAD), v_rows, lam)
    o = rmsnorm(o, lw['g_subln']) * (1.0 - lam_init)
    b_att = o.reshape(B, T, D_ATT) @ lw['w_br_att']
    g_lru, g_att = jnp.split(jax.nn.sigmoid(gate_logits), 2, axis=-1)
    h = h + (g_lru * b_lru + g_att * b_att) @ lw['w_o']
    n2 = rmsnorm(h, lw['g_ffn'])
    h = h + (jax.nn.silu(n2 @ lw['w_gate']) * (n2 @ lw['w_up'])) @ lw['w_down']
    pg = jax.nn.sigmoid(rmsnorm(h, lw['g_ple']) @ lw['w_ple_gate'])
    h = h + (p_l @ lw['w_ple']) * pg
    return h, k_rows, v_rows, new_buf, h_last


def trunk(x, p, conv0, h0, make_attend, lws, g_final):
    h = x
    ks, vs, cs, hs = [], [], [], []
    for l in range(DEPTH):
        h, k, v, c, hl = decoder_layer(h, p[l], conv0[l], h0[l], make_attend(l), lws[l])
        ks.append(k)
        vs.append(v)
        cs.append(c)
        hs.append(hl)
    return rmsnorm(h, g_final), jnp.stack(ks), jnp.stack(vs), jnp.stack(cs), jnp.stack(hs)


def setup_inputs(seed: int = 0) -> dict:
    key = jax.random.key(seed)
    keys = iter(jax.random.split(key, 48))
    f32 = jnp.float32
    n_pages = PAST_LEN // PAGE_SIZE
    n_used = DEC_BATCH * n_pages
    n_pool = n_used + n_used // 4

    def nrm(shape, scale):
        return jax.random.normal(next(keys), shape, f32) * scale

    def gain(shape):
        return 1.0 + nrm(shape, 0.05)

    page_table = jax.random.permutation(next(keys), n_pool)[:n_used].reshape(DEC_BATCH, n_pages).astype(jnp.int32)
    a0 = jax.random.uniform(next(keys), (DEPTH, D_RNN), f32, 0.9, 0.999)
    s0 = a0 ** (1.0 / LRU_C)
    lru_lambda = jnp.log(s0) - jnp.log1p(-s0)
    return {
        'x_prompt': nrm((BATCH, SEQ, D_MODEL), 1.0),
        'x_sample': nrm((DEC_BATCH, DEC_SEQ, D_MODEL), 1.0),
        'p_prompt': nrm((DEPTH, BATCH, SEQ, D_PLE), 1.0),
        'p_sample': nrm((DEPTH, DEC_BATCH, DEC_SEQ, D_PLE), 1.0),
        'cache_k': nrm((DEPTH, n_pool, PAGE_SIZE, N_KV_HEADS, 2 * D_HEAD), 1.0),
        'cache_v': nrm((DEPTH, n_pool, PAGE_SIZE, N_KV_HEADS, D_VHEAD), 1.0),
        'page_table': page_table,
        'state_conv': nrm((DEPTH, DEC_BATCH, CONV_W - 1, D_RNN), 1.0),
        'state_h': nrm((DEPTH, DEC_BATCH, D_RNN), 0.5),
        'w_in': nrm((DEPTH, D_MODEL, D_IN), D_MODEL ** -0.5),
        'g_mix': gain((DEPTH, D_MODEL)),
        'conv_w': nrm((DEPTH, CONV_W, D_RNN), CONV_W ** -0.5),
        'conv_b': nrm((DEPTH, D_RNN), 0.02),
        'w_a': nrm((DEPTH, N_LRU_BLOCKS, LRU_BLOCK, LRU_BLOCK), LRU_BLOCK ** -0.5),
        'b_a': nrm((DEPTH, D_RNN), 0.02),
        'w_i': nrm((DEPTH, N_LRU_BLOCKS, LRU_BLOCK, LRU_BLOCK), LRU_BLOCK ** -0.5),
        'b_i': nrm((DEPTH, D_RNN), 0.02),
        'lru_lambda': lru_lambda,
        'lam_q1': nrm((DEPTH, D_HEAD), 0.1),
        'lam_k1': nrm((DEPTH, D_HEAD), 0.1),
        'lam_q2': nrm((DEPTH, D_HEAD), 0.1),
        'lam_k2': nrm((DEPTH, D_HEAD), 0.1),
        'g_subln': gain((DEPTH, D_VHEAD)),
        'w_br_lru': nrm((DEPTH, D_RNN, D_MODEL), D_RNN ** -0.5),
        'w_br_att': nrm((DEPTH, D_ATT, D_MODEL), D_ATT ** -0.5),
        'w_o': nrm((DEPTH, D_MODEL, D_MODEL), D_MODEL ** -0.5),
        'g_ffn': gain((DEPTH, D_MODEL)),
        'w_gate': nrm((DEPTH, D_MODEL, D_FF), D_MODEL ** -0.5),
        'w_up': nrm((DEPTH, D_MODEL, D_FF), D_MODEL ** -0.5),
        'w_down': nrm((DEPTH, D_FF, D_MODEL), D_FF ** -0.5),
        'g_ple': gain((DEPTH, D_MODEL)),
        'w_ple_gate': nrm((DEPTH, D_MODEL, D_MODEL), D_MODEL ** -0.5),
        'w_ple': nrm((DEPTH, D_PLE, D_MODEL), D_PLE ** -0.5),
        'g_final': gain((D_MODEL,)),
    }


def reference(x_prompt, x_sample, p_prompt, p_sample, cache_k, cache_v, page_table, state_conv, state_h,
              w_in, g_mix, conv_w, conv_b, w_a, b_a, w_i, b_i, lru_lambda, lam_q1, lam_k1, lam_q2, lam_k2,
              g_subln, w_br_lru, w_br_att, w_o, g_ffn, w_gate, w_up, w_down, g_ple, w_ple_gate, w_ple, g_final):
    slopes = alibi_slopes()
    lws = [dict(w_in=w_in[l], g_mix=g_mix[l], conv_w=conv_w[l], conv_b=conv_b[l], w_a=w_a[l], b_a=b_a[l],
                w_i=w_i[l], b_i=b_i[l], lru_lambda=lru_lambda[l], lam_q1=lam_q1[l], lam_k1=lam_k1[l],
                lam_q2=lam_q2[l], lam_k2=lam_k2[l], g_subln=g_subln[l], w_br_lru=w_br_lru[l],
                w_br_att=w_br_att[l], w_o=w_o[l], g_ffn=g_ffn[l], w_gate=w_gate[l], w_up=w_up[l],
                w_down=w_down[l], g_ple=g_ple[l], w_ple_gate=w_ple_gate[l], w_ple=w_ple[l],
                lam_init=0.8 - 0.6 * math.exp(-0.3 * l))
           for l in range(DEPTH)]

    def prompt_attend(l):
        return lambda q, k, v, lam: attend_prompt(q, k, v, lam, slopes)

    def sample_attend(l):
        n_seq = x_sample.shape[0]
        k_past = cache_k[l, page_table].reshape(n_seq, -1, N_KV_HEADS, 2, D_HEAD)
        v_past = cache_v[l, page_table].reshape(n_seq, -1, N_KV_HEADS, D_VHEAD)
        return lambda q, k, v, lam: attend_sample(q, k, v, k_past, v_past, lam, slopes)

    conv0 = jnp.zeros((DEPTH, x_prompt.shape[0], CONV_W - 1, D_RNN), x_prompt.dtype)
    h0 = jnp.zeros((DEPTH, x_prompt.shape[0], D_RNN), x_prompt.dtype)
    y_prompt, k_prompt, v_prompt, conv_prompt, h_prompt = trunk(x_prompt, p_prompt, conv0, h0, prompt_attend, lws, g_final)
    y_sample, k_sample, v_sample, conv_sample, h_sample = trunk(x_sample, p_sample, state_conv, state_h, sample_attend, lws, g_final)
    return (y_prompt, y_sample, k_prompt, v_prompt, conv_prompt, h_prompt, k_sample, v_sample, conv_sample, h_sample)
```

```python
import functools
import math

import jax
import jax.numpy as jnp
from jax import lax
from jax.experimental import pallas as pl
from jax.experimental.pallas import tpu as pltpu

_F32 = jnp.float32
_BF16 = jnp.bfloat16

N_HEADS = 8
N_KV_HEADS = 4
GROUP = N_HEADS // N_KV_HEADS
D_HEAD = 64
D_VHEAD = 2 * D_HEAD
N_STREAMS = N_HEADS * 2
N_LRU_BLOCKS = 8
CONV_W = 4
LRU_C = 8.0
EPS = 1e-6
SLOPES = tuple(2.0 ** (-8.0 * (i + 1) / N_HEADS) for i in range(N_HEADS))
NEG_BIG = -1e30

LANES = 128
SUBLANES = 8
VMEM_LIMIT_BYTES = 56 * 1024 * 1024

ROW_BLOCK = 256
ATTN_BLOCK = 512
POS_SPLIT = 64


def _lam_init(layer):
    return 0.8 - 0.6 * math.exp(-0.3 * layer)


def _rms(x, g):
    return x * lax.rsqrt(jnp.mean(x * x, axis=-1, keepdims=True) + EPS) * g


def _dot(a, b):
    return jnp.dot(a, b, preferred_element_type=_F32)


def _dot_nt(a, b):
    return lax.dot_general(a, b, (((1,), (1,)), ((), ())), preferred_element_type=_F32)


def _softplus(x):
    return jnp.maximum(x, 0.0) + jnp.log1p(jnp.exp(-jnp.abs(x)))


def _lam_value(q1, k1, q2, k2, lam_init):
    s1 = jnp.sum(q1 * k1, axis=-1, keepdims=True)
    s2 = jnp.sum(q2 * k2, axis=-1, keepdims=True)
    return jnp.exp(s1) - jnp.exp(s2) + lam_init


def _pre_kernel(x_ref, cs_ref, hs_ref, gmix_ref, w_in_ref, cw_ref, cb_ref, wai_ref, ba_ref, bi_ref,
                lam_ref, wbr_ref, *rest, stride, rows, pad, is_prompt):
    if is_prompt:
        (q_ref, k_ref, v_ref, m_ref, g_ref, co_ref, ho_ref, kr_ref, vb_ref,
         up_scr, a_scr, x_scr, h_scr) = rest
    else:
        (q_ref, k_ref, v_ref, m_ref, g_ref, co_ref, ho_ref,
         up_scr, a_scr, x_scr, h_scr) = rest
    d = x_ref.shape[-1]
    i = pl.program_id(1)
    hist = (CONV_W - 1) * stride

    n = _rms(x_ref[...], gmix_ref[...]).astype(_BF16)

    @pl.when(i == 0)
    def _():
        up_scr[pl.ds(pad - hist, hist), :] = cs_ref[...]
        h_scr[...] = hs_ref[...]

    up_scr[pl.ds(pad, rows), :] = _dot(n, w_in_ref[:, 0:d])
    uc = up_scr[pl.ds(pad - hist, rows), :] * cw_ref[0:1, :]
    for j in range(1, CONV_W):
        uc = uc + up_scr[pl.ds(pad - hist + j * stride, rows), :] * cw_ref[j:j + 1, :]
    uc = uc + cb_ref[...]
    tail = up_scr[pl.ds(pad + rows - hist, hist), :]
    co_ref[...] = tail
    up_scr[pl.ds(pad - hist, hist), :] = tail

    ucb = uc.astype(_BF16)
    blk = d // N_LRU_BLOCKS
    for nb in range(N_LRU_BLOCKS):
        sl = slice(nb * blk, (nb + 1) * blk)
        gates = _dot(ucb[:, sl], wai_ref[nb])
        r = jax.nn.sigmoid(gates[:, :blk] + ba_ref[:, sl])
        ig = jax.nn.sigmoid(gates[:, blk:] + bi_ref[:, sl])
        a = jnp.exp(-LRU_C * r * _softplus(-lam_ref[:, sl]))
        a_scr[:, sl] = a
        x_scr[:, sl] = jnp.sqrt(1.0 - a * a) * (ig * uc[:, sl])

    def step(t, h):
        sl = pl.ds(pl.multiple_of(t * stride, stride), stride)
        h = a_scr[sl, :] * h + x_scr[sl, :]
        x_scr[sl, :] = h
        return h

    h_last = lax.fori_loop(0, rows // stride, step, h_scr[...], unroll=min(8, rows // stride))
    h_scr[...] = h_last
    ho_ref[...] = h_last

    gl = _dot(n, w_in_ref[:, d:2 * d])
    b_lru = _dot((x_scr[...] * jax.nn.gelu(gl)).astype(_BF16), wbr_ref[...])

    o_q = 2 * d
    o_k = o_q + N_HEADS * 2 * D_HEAD
    o_v = o_k + N_KV_HEADS * 2 * D_HEAD
    o_g = o_v + N_KV_HEADS * D_VHEAD
    lane = lax.broadcasted_iota(jnp.int32, (1, LANES), 1)
    low = lane < D_HEAD
    qz = _dot(n, w_in_ref[:, o_q:o_k]) * (D_HEAD ** -0.5)
    kz = _dot(n, w_in_ref[:, o_k:o_v])
    vz = _dot(n, w_in_ref[:, o_v:o_g])
    k_ref[...] = kz
    v_ref[...] = vz
    if is_prompt:
        vb_ref[...] = vz.astype(_BF16)
        pos = i * rows + lax.broadcasted_iota(jnp.int32, (rows, 1), 0)
        hi = (pos // POS_SPLIT).astype(_F32)
        lo = (pos % POS_SPLIT).astype(_F32)
        k_extra = jnp.where(lane == D_HEAD, hi, jnp.where(lane == D_HEAD + 1, lo, 0.0))
        for h in range(N_KV_HEADS):
            chunk = kz[:, h * LANES:(h + 1) * LANES]
            kr_ref[:, (2 * h) * LANES:(2 * h + 1) * LANES] = jnp.where(low, chunk, k_extra).astype(_BF16)
            kr_ref[:, (2 * h + 1) * LANES:(2 * h + 2) * LANES] = jnp.where(
                low, pltpu.roll(chunk, D_HEAD, 1), k_extra).astype(_BF16)
    for hg in range(N_HEADS):
        chunk = qz[:, hg * LANES:(hg + 1) * LANES]
        if is_prompt:
            q_extra = jnp.where(lane == D_HEAD, SLOPES[hg] * POS_SPLIT,
                                jnp.where(lane == D_HEAD + 1, SLOPES[hg], 0.0))
            c0 = jnp.where(low, chunk, q_extra)
            c1 = jnp.where(low, pltpu.roll(chunk, D_HEAD, 1), q_extra)
        else:
            c0 = jnp.where(low, chunk, 0.0)
            c1 = jnp.where(low, 0.0, chunk)
        q_ref[:, (2 * hg) * LANES:(2 * hg + 1) * LANES] = c0.astype(_BF16)
        q_ref[:, (2 * hg + 1) * LANES:(2 * hg + 2) * LANES] = c1.astype(_BF16)

    m_ref[...] = jax.nn.sigmoid(_dot(n, w_in_ref[:, o_g:o_g + d])) * b_lru
    g_ref[...] = jax.nn.sigmoid(_dot(n, w_in_ref[:, o_g + d:o_g + 2 * d]))


def _const_spec(shape, layer):
    nd = len(shape)
    return pl.BlockSpec((None,) + tuple(shape[1:]), lambda *_: (layer,) + (0,) * (nd - 1),
                        pipeline_mode=pl.Buffered(1))


def _pre_call(x, conv_state, h_state, w, layer, *, stride, rows, is_prompt):
    b, r, d = x.shape
    assert r % rows == 0 and rows % stride == 0 and rows >= (CONV_W - 1) * stride
    hist = (CONV_W - 1) * stride
    pad = -(-hist // SUBLANES) * SUBLANES
    grid = (b, r // rows)
    row_spec = lambda c: pl.BlockSpec((None, rows, c), lambda bi, i: (bi, i, 0))
    state_spec = lambda c: pl.BlockSpec((None, c, d), lambda bi, i: (bi, 0, 0))
    in_specs = [
        row_spec(d), state_spec(hist), state_spec(stride),
        _const_spec(w['g_mix'].shape, layer), _const_spec(w['w_in'].shape, layer),
        _const_spec(w['conv_w'].shape, layer), _const_spec(w['conv_b'].shape, layer),
        _const_spec(w['w_ai'].shape, layer), _const_spec(w['b_a'].shape, layer),
        _const_spec(w['b_i'].shape, layer), _const_spec(w['lru_lambda'].shape, layer),
        _const_spec(w['w_br_lru'].shape, layer),
    ]
    dq = N_STREAMS * LANES
    dk = N_KV_HEADS * 2 * D_HEAD
    dv = N_KV_HEADS * D_VHEAD
    out_shape = [
        jax.ShapeDtypeStruct((b, r, dq), _BF16), jax.ShapeDtypeStruct((b, r, dk), _F32),
        jax.ShapeDtypeStruct((b, r, dv), _F32), jax.ShapeDtypeStruct((b, r, d), _F32),
        jax.ShapeDtypeStruct((b, r, d), _F32), jax.ShapeDtypeStruct((b, hist, d), _F32),
        jax.ShapeDtypeStruct((b, stride, d), _F32),
    ]
    out_specs = [row_spec(dq), row_spec(dk), row_spec(dv), row_spec(d), row_spec(d),
                 state_spec(hist), state_spec(stride)]
    if is_prompt:
        out_shape += [jax.ShapeDtypeStruct((b, r, 2 * dk), _BF16), jax.ShapeDtypeStruct((b, r, dv), _BF16)]
        out_specs += [row_spec(2 * dk), row_spec(dv)]
    kern = functools.partial(_pre_kernel, stride=stride, rows=rows, pad=pad, is_prompt=is_prompt)
    return pl.pallas_call(
        kern, grid=grid, in_specs=in_specs, out_specs=out_specs, out_shape=out_shape,
        scratch_shapes=[pltpu.VMEM((pad + rows, d), _F32), pltpu.VMEM((rows, d), _F32),
                        pltpu.VMEM((rows, d), _F32), pltpu.VMEM((stride, d), _F32)],
        compiler_params=pltpu.CompilerParams(dimension_semantics=("arbitrary", "arbitrary"),
                                             vmem_limit_bytes=VMEM_LIMIT_BYTES),
        name="pre_prompt" if is_prompt else "pre_sample",
    )(x, conv_state, h_state, w['g_mix'], w['w_in'], w['conv_w'], w['conv_b'], w['w_ai'], w['b_a'],
      w['b_i'], w['lru_lambda'], w['w_br_lru'])


def _subln(o0, l0, o1, l1, lam, gain, lam_init):
    o = o0 / l0 - lam * (o1 / l1)
    return _rms(o, gain) * (1.0 - lam_init)


def _attn_prompt_kernel(qt_ref, kt_ref, q_ref, k_ref, v_ref, lq1_ref, lk1_ref, lq2_ref, lk2_ref, gs_ref,
                        o_ref, m_scr, l_scr, acc_scr, *, lam_init, blk):
    s_idx = pl.program_id(1)
    qi = qt_ref[s_idx]
    ki = kt_ref[s_idx]

    @pl.when(ki == 0)
    def _():
        m_scr[...] = jnp.full(m_scr.shape, NEG_BIG, _F32)
        l_scr[...] = jnp.zeros(l_scr.shape, _F32)
        acc_scr[...] = jnp.zeros(acc_scr.shape, _F32)

    def update(masked):
        if masked:
            row = lax.broadcasted_iota(jnp.int32, (blk, blk), 0)
            col = lax.broadcasted_iota(jnp.int32, (blk, blk), 1)
            keep = col <= row
        for h in range(N_KV_HEADS):
            v = v_ref[:, h * LANES:(h + 1) * LANES]
            for c in range(2):
                k = k_ref[:, (2 * h + c) * LANES:(2 * h + c + 1) * LANES]
                for g in range(GROUP):
                    st = (h * GROUP + g) * 2 + c
                    s = _dot_nt(q_ref[:, st * LANES:(st + 1) * LANES], k)
                    if masked:
                        s = jnp.where(keep, s, NEG_BIG)
                    m_prev = m_scr[st]
                    m_new = jnp.maximum(m_prev, jnp.max(s, axis=-1, keepdims=True))
                    alpha = jnp.exp(m_prev - m_new)
                    p = jnp.exp(s - m_new)
                    l_scr[st] = alpha * l_scr[st] + jnp.sum(p, axis=-1, keepdims=True)
                    acc_scr[st] = alpha * acc_scr[st] + _dot(p.astype(_BF16), v)
                    m_scr[st] = m_new

    @pl.when(ki < qi)
    def _():
        update(False)

    @pl.when(ki == qi)
    def _():
        update(True)
        lam = _lam_value(lq1_ref[...], lk1_ref[...], lq2_ref[...], lk2_ref[...], lam_init)
        for hg in range(N_HEADS):
            o = _subln(acc_scr[2 * hg], l_scr[2 * hg], acc_scr[2 * hg + 1], l_scr[2 * hg + 1],
                       lam, gs_ref[...], lam_init)
            o_ref[:, hg * LANES:(hg + 1) * LANES] = o.astype(o_ref.dtype)


def _attn_prompt_call(q, k, v, w, layer):
    b, t, _ = q.shape
    blk = min(ATTN_BLOCK, t)
    assert t % blk == 0 and t <= POS_SPLIT * 256
    nq = t // blk
    pairs = [(qi, ki) for qi in range(nq) for ki in range(qi + 1)]
    qt = jnp.asarray([p[0] for p in pairs], jnp.int32)
    kt = jnp.asarray([p[1] for p in pairs], jnp.int32)
    small = lambda a: pl.BlockSpec((None,) + tuple(a.shape[1:]), lambda bi, s, qt, kt: (layer, 0, 0))
    grid_spec = pltpu.PrefetchScalarGridSpec(
        num_scalar_prefetch=2, grid=(b, len(pairs)),
        in_specs=[
            pl.BlockSpec((None, blk, q.shape[-1]), lambda bi, s, qt, kt: (bi, qt[s], 0)),
            pl.BlockSpec((None, blk, k.shape[-1]), lambda bi, s, qt, kt: (bi, kt[s], 0)),
            pl.BlockSpec((None, blk, v.shape[-1]), lambda bi, s, qt, kt: (bi, kt[s], 0)),
            small(w['lam_q1']), small(w['lam_k1']), small(w['lam_q2']), small(w['lam_k2']),
            small(w['g_subln']),
        ],
        out_specs=pl.BlockSpec((None, blk, N_HEADS * D_VHEAD), lambda bi, s, qt, kt: (bi, qt[s], 0)),
        scratch_shapes=[pltpu.VMEM((N_STREAMS, blk, 1), _F32), pltpu.VMEM((N_STREAMS, blk, 1), _F32),
                        pltpu.VMEM((N_STREAMS, blk, D_VHEAD), _F32)],
    )
    kern = functools.partial(_attn_prompt_kernel, lam_init=_lam_init(layer), blk=blk)
    return pl.pallas_call(
        kern, grid_spec=grid_spec,
        out_shape=jax.ShapeDtypeStruct((b, t, N_HEADS * D_VHEAD), _BF16),
        compiler_params=pltpu.CompilerParams(dimension_semantics=("arbitrary", "arbitrary"),
                                             vmem_limit_bytes=VMEM_LIMIT_BYTES),
        name="attn_prompt",
    )(qt, kt, q, k, v, w['lam_q1'], w['lam_k1'], w['lam_q2'], w['lam_k2'], w['g_subln'])


def _attn_sample_kernel(pt_ref, q_ref, kn_ref, vn_ref, lq1_ref, lk1_ref, lq2_ref, lk2_ref, gs_ref, *rest,
                        lam_init, n_pages, page, n_new):
    k_pages = rest[:n_pages]
    v_pages = rest[n_pages:2 * n_pages]
    o_ref = rest[2 * n_pages]
    del pt_ref
    past = n_pages * page
    n_rows = q_ref.shape[0]
    new_pad = kn_ref.shape[0]
    r_idx = lax.broadcasted_iota(jnp.int32, (n_rows, 1), 0)
    t_row = r_idx % n_new
    q_pos = (past + t_row).astype(_F32)
    k_pos_past = lax.broadcasted_iota(jnp.int32, (1, past), 1).astype(_F32)
    j_new = lax.broadcasted_iota(jnp.int32, (1, new_pad), 1)
    k_pos_new = (past + j_new).astype(_F32)
    keep_new = j_new <= t_row
    lam = _lam_value(lq1_ref[...], lk1_ref[...], lq2_ref[...], lk2_ref[...], lam_init)
    for h in range(N_KV_HEADS):
        hs = slice(h * LANES, (h + 1) * LANES)
        q = q_ref[:, hs]
        slope = jnp.where(r_idx < 2 * n_new, SLOPES[h * GROUP], SLOPES[h * GROUP + 1])
        s_past = jnp.concatenate([_dot_nt(q, kp[:, hs].astype(_BF16)) for kp in k_pages], axis=1)
        s_past = s_past - slope * (q_pos - k_pos_past)
        s_new = _dot_nt(q, kn_ref[:, hs].astype(_BF16))
        s_new = jnp.where(keep_new, s_new - slope * (q_pos - k_pos_new), NEG_BIG)
        m = jnp.maximum(jnp.max(s_past, axis=-1, keepdims=True), jnp.max(s_new, axis=-1, keepdims=True))
        p_past = jnp.exp(s_past - m)
        p_new = jnp.exp(s_new - m)
        l = jnp.sum(p_past, axis=-1, keepdims=True) + jnp.sum(p_new, axis=-1, keepdims=True)
        acc = _dot(p_new.astype(_BF16), vn_ref[:, hs].astype(_BF16))
        for j, vp in enumerate(v_pages):
            acc = acc + _dot(p_past[:, j * page:(j + 1) * page].astype(_BF16), vp[:, hs].astype(_BF16))
        on = acc / l
        halves = []
        for g in range(GROUP):
            base = g * 2 * n_new
            halves.append(on[base:base + n_new] - lam * on[base + n_new:base + 2 * n_new])
        o = jnp.concatenate(halves, axis=0)
        o_ref[:, hs] = (_rms(o, gs_ref[...]) * (1.0 - lam_init)).astype(o_ref.dtype)


def _attn_sample_call(q, k_new, v_new, cache_k, cache_v, page_table, w, layer):
    n, n_rows, dq = q.shape
    n_new = n_rows // (2 * GROUP)
    n_pages = page_table.shape[1]
    page = cache_k.shape[2]
    small = lambda a: pl.BlockSpec((None,) + tuple(a.shape[1:]), lambda bi, pt: (layer, 0, 0))
    page_spec = lambda j: pl.BlockSpec((None, None, page, cache_k.shape[-1]),
                                       lambda bi, pt: (layer, pt[bi * n_pages + j], 0, 0))
    grid_spec = pltpu.PrefetchScalarGridSpec(
        num_scalar_prefetch=1, grid=(n,),
        in_specs=[pl.BlockSpec((None, n_rows, dq), lambda bi, pt: (bi, 0, 0)),
                  pl.BlockSpec((None,) + tuple(k_new.shape[1:]), lambda bi, pt: (bi, 0, 0)),
                  pl.BlockSpec((None,) + tuple(v_new.shape[1:]), lambda bi, pt: (bi, 0, 0)),
                  small(w['lam_q1']), small(w['lam_k1']), small(w['lam_q2']), small(w['lam_k2']),
                  small(w['g_subln'])]
                 + [page_spec(j) for j in range(n_pages)] + [page_spec(j) for j in range(n_pages)],
        out_specs=pl.BlockSpec((None, GROUP * n_new, dq), lambda bi, pt: (bi, 0, 0)),
    )
    kern = functools.partial(_attn_sample_kernel, lam_init=_lam_init(layer), n_pages=n_pages, page=page,
                             n_new=n_new)
    return pl.pallas_call(
        kern, grid_spec=grid_spec,
        out_shape=jax.ShapeDtypeStruct((n, GROUP * n_new, dq), _BF16),
        compiler_params=pltpu.CompilerParams(dimension_semantics=("arbitrary",),
                                             vmem_limit_bytes=VMEM_LIMIT_BYTES),
        name="attn_sample",
    )(page_table.reshape(-1), q, k_new, v_new, w['lam_q1'], w['lam_k1'], w['lam_q2'], w['lam_k2'],
      w['g_subln'], *([cache_k] * n_pages), *([cache_v] * n_pages))


def _post_kernel(h_ref, m_ref, g_ref, o_ref, p_ref, wba_ref, wo_ref, gffn_ref, wg_ref, wu_ref, wd_ref,
                 gple_ref, wpg_ref, wp_ref, gfin_ref, out_ref, *, is_last):
    mix = m_ref[...] + g_ref[...] * _dot(o_ref[...], wba_ref[...])
    h = h_ref[...] + _dot(mix.astype(_BF16), wo_ref[...])
    n2 = _rms(h, gffn_ref[...]).astype(_BF16)
    act = jax.nn.silu(_dot(n2, wg_ref[...])) * _dot(n2, wu_ref[...])
    h = h + _dot(act.astype(_BF16), wd_ref[...])
    n3 = _rms(h, gple_ref[...]).astype(_BF16)
    pg = jax.nn.sigmoid(_dot(n3, wpg_ref[...]))
    h = h + _dot(p_ref[...].astype(_BF16), wp_ref[...]) * pg
    out_ref[...] = _rms(h, gfin_ref[...]) if is_last else h


def _post_call(h, m_lru, g_att, o, p, w, layer, *, is_last):
    r, d = h.shape
    rows = min(ROW_BLOCK, r)
    assert r % rows == 0
    row_spec = lambda c: pl.BlockSpec((rows, c), lambda i: (i, 0))
    names = ['w_br_att', 'w_o', 'g_ffn', 'w_gate', 'w_up', 'w_down', 'g_ple', 'w_ple_gate', 'w_ple']
    kern = functools.partial(_post_kernel, is_last=is_last)
    return pl.pallas_call(
        kern, grid=(r // rows,),
        in_specs=[row_spec(d), row_spec(d), row_spec(d), row_spec(o.shape[-1]), row_spec(p.shape[-1])]
                 + [_const_spec(w[nm].shape, layer) for nm in names]
                 + [pl.BlockSpec(w['g_final'].shape, lambda i: (0, 0), pipeline_mode=pl.Buffered(1))],
        out_specs=row_spec(d),
        out_shape=jax.ShapeDtypeStruct((r, d), _F32),
        compiler_params=pltpu.CompilerParams(dimension_semantics=("arbitrary",),
                                             vmem_limit_bytes=VMEM_LIMIT_BYTES),
        name="post",
    )(h, m_lru, g_att, o, p, *[w[nm] for nm in names], w['g_final'])


def kernel(x_prompt, x_sample, p_prompt, p_sample, cache_k, cache_v, page_table, state_conv, state_h,
           w_in, g_mix, conv_w, conv_b, w_a, b_a, w_i, b_i, lru_lambda, lam_q1, lam_k1, lam_q2, lam_k2,
           g_subln, w_br_lru, w_br_att, w_o, g_ffn, w_gate, w_up, w_down, g_ple, w_ple_gate, w_ple, g_final):
    depth, d = g_mix.shape
    bp, t, _ = x_prompt.shape
    ns, ts, _ = x_sample.shape
    hist = CONV_W - 1
    row3 = lambda a: a.reshape(depth, 1, a.shape[-1])
    w = dict(
        w_in=w_in.astype(_BF16), g_mix=row3(g_mix), conv_w=conv_w, conv_b=row3(conv_b),
        w_ai=jnp.concatenate([w_a, w_i], axis=-1).astype(_BF16), b_a=row3(b_a), b_i=row3(b_i),
        lru_lambda=row3(lru_lambda), w_br_lru=w_br_lru.astype(_BF16),
        lam_q1=row3(lam_q1), lam_k1=row3(lam_k1), lam_q2=row3(lam_q2), lam_k2=row3(lam_k2),
        g_subln=row3(g_subln), w_br_att=w_br_att.astype(_BF16), w_o=w_o.astype(_BF16), g_ffn=row3(g_ffn),
        w_gate=w_gate.astype(_BF16), w_up=w_up.astype(_BF16), w_down=w_down.astype(_BF16),
        g_ple=row3(g_ple), w_ple_gate=w_ple_gate.astype(_BF16), w_ple=w_ple.astype(_BF16),
        g_final=g_final.reshape(1, d),
    )
    ck = cache_k.reshape(cache_k.shape[0], cache_k.shape[1], cache_k.shape[2], -1)
    cv = cache_v.reshape(cache_v.shape[0], cache_v.shape[1], cache_v.shape[2], -1)
    dk = N_KV_HEADS * 2 * D_HEAD
    dv = N_KV_HEADS * D_VHEAD
    new_pad = -(-ts // SUBLANES) * SUBLANES

    tm = lambda a: jnp.swapaxes(a, -3, -2)
    hp = x_prompt
    hs = tm(x_sample).reshape(1, ts * ns, d)
    cs_all = tm(state_conv).reshape(depth, 1, hist * ns, d)
    ps_all = tm(p_sample).reshape(depth, ts * ns, -1)
    zero_c = jnp.zeros((bp, hist, d), _F32)
    zero_h = jnp.zeros((bp, 1, d), _F32)
    rows_p = min(ROW_BLOCK, t)

    ks_p, vs_p, cs_p, hs_p, ks_s, vs_s, cs_s, hs_s = ([] for _ in range(8))
    for l in range(depth):
        last = l == depth - 1
        q, k, v, m_lru, g_att, c_out, h_out, k_rdy, v_bf = _pre_call(
            hp, zero_c, zero_h, w, l, stride=1, rows=rows_p, is_prompt=True)
        o = _attn_prompt_call(q, k_rdy, v_bf, w, l)
        hp = _post_call(hp.reshape(bp * t, d), m_lru.reshape(bp * t, d), g_att.reshape(bp * t, d),
                        o.reshape(bp * t, -1), p_prompt[l].reshape(bp * t, -1), w, l,
                        is_last=last).reshape(bp, t, d)
        ks_p.append(k.reshape(bp, t, N_KV_HEADS, 2 * D_HEAD))
        vs_p.append(v.reshape(bp, t, N_KV_HEADS, D_VHEAD))
        cs_p.append(c_out)
        hs_p.append(h_out.reshape(bp, d))
        q, k, v, m_lru, g_att, c_out, h_out = _pre_call(
            hs, cs_all[l], state_h[l][None], w, l, stride=ns, rows=ts * ns, is_prompt=False)
        qs = q.reshape(ts, ns, N_KV_HEADS, GROUP, 2, LANES).transpose(1, 3, 4, 0, 2, 5)
        qs = qs.reshape(ns, GROUP * 2 * ts, N_KV_HEADS * LANES)
        k_sm = tm(k.reshape(ts, ns, dk))
        v_sm = tm(v.reshape(ts, ns, dv))
        padn = lambda a: jnp.pad(a, ((0, 0), (0, new_pad - ts), (0, 0)))
        o = _attn_sample_call(qs, padn(k_sm), padn(v_sm), ck, cv, page_table, w, l)
        o = o.reshape(ns, GROUP, ts, N_KV_HEADS, D_VHEAD).transpose(2, 0, 3, 1, 4).reshape(ts * ns, -1)
        hs = _post_call(hs.reshape(ts * ns, d), m_lru.reshape(ts * ns, d), g_att.reshape(ts * ns, d),
                        o, ps_all[l], w, l, is_last=last).reshape(1, ts * ns, d)
        ks_s.append(k_sm.reshape(ns, ts, N_KV_HEADS, 2 * D_HEAD))
        vs_s.append(v_sm.reshape(ns, ts, N_KV_HEADS, D_VHEAD))
        cs_s.append(tm(c_out.reshape(hist, ns, d)))
        hs_s.append(h_out.reshape(ns, d))

    y_sample = tm(hs.reshape(ts, ns, d))
    return (hp, y_sample, jnp.stack(ks_p), jnp.stack(vs_p), jnp.stack(cs_p), jnp.stack(hs_p),
            jnp.stack(ks_s), jnp.stack(vs_s), jnp.stack(cs_s), jnp.stack(hs_s))
```

```python
import functools
import math

import jax
import jax.numpy as jnp
from jax import lax
from jax.experimental import pallas as pl
from jax.experimental.pallas import tpu as pltpu

_F32 = jnp.float32
_BF16 = jnp.bfloat16

N_HEADS = 8
N_KV_HEADS = 4
GROUP = N_HEADS // N_KV_HEADS
D_HEAD = 64
D_VHEAD = 2 * D_HEAD
N_STREAMS = N_HEADS * 2
N_LRU_BLOCKS = 8
CONV_W = 4
LRU_C = 8.0
EPS = 1e-6
SLOPES = tuple(2.0 ** (-8.0 * (i + 1) / N_HEADS) for i in range(N_HEADS))
NEG_BIG = -1e30

LANES = 128
SUBLANES = 8
VMEM_LIMIT_BYTES = 56 * 1024 * 1024

ROW_BLOCK = 256
ATTN_Q_BLOCK = 256
ATTN_K_BLOCK = 1024
POS_SPLIT = 64


def _lam_init(layer):
    return 0.8 - 0.6 * math.exp(-0.3 * layer)


def _rms(x, g):
    return x * lax.rsqrt(jnp.mean(x * x, axis=-1, keepdims=True) + EPS) * g


def _dot(a, b):
    return jnp.dot(a, b, preferred_element_type=_F32)


def _dot_nt(a, b):
    return lax.dot_general(a, b, (((1,), (1,)), ((), ())), preferred_element_type=_F32)


def _softplus(x):
    return jnp.maximum(x, 0.0) + jnp.log1p(jnp.exp(-jnp.abs(x)))


def _lam_value(q1, k1, q2, k2, lam_init):
    s1 = jnp.sum(q1 * k1, axis=-1, keepdims=True)
    s2 = jnp.sum(q2 * k2, axis=-1, keepdims=True)
    return jnp.exp(s1) - jnp.exp(s2) + lam_init


def _pre_kernel(x_ref, cs_ref, hs_ref, gmix_ref, w_in_ref, cw_ref, cb_ref, wai_ref, ba_ref, bi_ref,
                lam_ref, wbr_ref, *rest, stride, rows, pad, is_prompt):
    if is_prompt:
        (q_ref, k_ref, v_ref, m_ref, g_ref, co_ref, ho_ref, kr_ref, va_ref,
         up_scr, a_scr, x_scr, h_scr) = rest
    else:
        (q_ref, k_ref, v_ref, m_ref, g_ref, co_ref, ho_ref,
         up_scr, a_scr, x_scr, h_scr) = rest
    d = x_ref.shape[-1]
    i = pl.program_id(1)
    hist = (CONV_W - 1) * stride

    n = _rms(x_ref[...], gmix_ref[...]).astype(_BF16)

    @pl.when(i == 0)
    def _():
        up_scr[pl.ds(pad - hist, hist), :] = cs_ref[...]
        h_scr[...] = hs_ref[...]

    up_scr[pl.ds(pad, rows), :] = _dot(n, w_in_ref[:, 0:d])
    uc = up_scr[pl.ds(pad - hist, rows), :] * cw_ref[0:1, :]
    for j in range(1, CONV_W):
        uc = uc + up_scr[pl.ds(pad - hist + j * stride, rows), :] * cw_ref[j:j + 1, :]
    uc = uc + cb_ref[...]
    tail = up_scr[pl.ds(pad + rows - hist, hist), :]
    co_ref[...] = tail
    up_scr[pl.ds(pad - hist, hist), :] = tail

    ucb = uc.astype(_BF16)
    blk = d // N_LRU_BLOCKS
    for nb in range(N_LRU_BLOCKS):
        sl = slice(nb * blk, (nb + 1) * blk)
        gates = _dot(ucb[:, sl], wai_ref[nb])
        r = jax.nn.sigmoid(gates[:, :blk] + ba_ref[:, sl])
        ig = jax.nn.sigmoid(gates[:, blk:] + bi_ref[:, sl])
        a = jnp.exp(-LRU_C * r * _softplus(-lam_ref[:, sl]))
        a_scr[:, sl] = a
        x_scr[:, sl] = jnp.sqrt(1.0 - a * a) * (ig * uc[:, sl])

    def step(t, h):
        sl = pl.ds(pl.multiple_of(t * stride, stride), stride)
        h = a_scr[sl, :] * h + x_scr[sl, :]
        x_scr[sl, :] = h
        return h

    h_last = lax.fori_loop(0, rows // stride, step, h_scr[...], unroll=min(8, rows // stride))
    h_scr[...] = h_last
    ho_ref[...] = h_last

    gl = _dot(n, w_in_ref[:, d:2 * d])
    b_lru = _dot((x_scr[...] * jax.nn.gelu(gl)).astype(_BF16), wbr_ref[...])

    o_q = 2 * d
    o_k = o_q + N_HEADS * 2 * D_HEAD
    o_v = o_k + N_KV_HEADS * 2 * D_HEAD
    o_g = o_v + N_KV_HEADS * D_VHEAD
    lane = lax.broadcasted_iota(jnp.int32, (1, LANES), 1)
    low = lane < D_HEAD
    qz = _dot(n, w_in_ref[:, o_q:o_k]) * (D_HEAD ** -0.5)
    kz = _dot(n, w_in_ref[:, o_k:o_v])
    vz = _dot(n, w_in_ref[:, o_v:o_g])
    k_ref[...] = kz
    v_ref[...] = vz
    if is_prompt:
        pos = i * rows + lax.broadcasted_iota(jnp.int32, (rows, 1), 0)
        hi = (pos // POS_SPLIT).astype(_F32)
        lo = (pos % POS_SPLIT).astype(_F32)
        k_extra = jnp.where(lane == D_HEAD, hi, jnp.where(lane == D_HEAD + 1, lo, 0.0))
        ones = jnp.ones((rows, LANES), _BF16)
        for h in range(N_KV_HEADS):
            chunk = kz[:, h * LANES:(h + 1) * LANES]
            kr_ref[2 * h] = jnp.where(low, chunk, k_extra).astype(_BF16)
            kr_ref[2 * h + 1] = jnp.where(low, pltpu.roll(chunk, D_HEAD, 1), k_extra).astype(_BF16)
            va_ref[h, :, 0:D_VHEAD] = vz[:, h * LANES:(h + 1) * LANES].astype(_BF16)
            va_ref[h, :, D_VHEAD:2 * D_VHEAD] = ones
    for hg in range(N_HEADS):
        chunk = qz[:, hg * LANES:(hg + 1) * LANES]
        if is_prompt:
            h, g = divmod(hg, GROUP)
            q_extra = jnp.where(lane == D_HEAD, SLOPES[hg] * POS_SPLIT,
                                jnp.where(lane == D_HEAD + 1, SLOPES[hg], 0.0))
            q_ref[(2 * h) * GROUP + g] = jnp.where(low, chunk, q_extra).astype(_BF16)
            q_ref[(2 * h + 1) * GROUP + g] = jnp.where(low, pltpu.roll(chunk, D_HEAD, 1), q_extra).astype(_BF16)
        else:
            q_ref[:, (2 * hg) * LANES:(2 * hg + 1) * LANES] = jnp.where(low, chunk, 0.0).astype(_BF16)
            q_ref[:, (2 * hg + 1) * LANES:(2 * hg + 2) * LANES] = jnp.where(low, 0.0, chunk).astype(_BF16)

    m_ref[...] = jax.nn.sigmoid(_dot(n, w_in_ref[:, o_g:o_g + d])) * b_lru
    g_ref[...] = jax.nn.sigmoid(_dot(n, w_in_ref[:, o_g + d:o_g + 2 * d]))


def _const_spec(shape, layer):
    nd = len(shape)
    return pl.BlockSpec((None,) + tuple(shape[1:]), lambda *_: (layer,) + (0,) * (nd - 1),
                        pipeline_mode=pl.Buffered(1))


def _pre_call(x, conv_state, h_state, w, layer, *, stride, rows, is_prompt):
    b, r, d = x.shape
    assert r % rows == 0 and rows % stride == 0 and rows >= (CONV_W - 1) * stride
    hist = (CONV_W - 1) * stride
    pad = -(-hist // SUBLANES) * SUBLANES
    grid = (b, r // rows)
    row_spec = lambda c: pl.BlockSpec((None, rows, c), lambda bi, i: (bi, i, 0))
    state_spec = lambda c: pl.BlockSpec((None, c, d), lambda bi, i: (bi, 0, 0))
    in_specs = [
        row_spec(d), state_spec(hist), state_spec(stride),
        _const_spec(w['g_mix'].shape, layer), _const_spec(w['w_in'].shape, layer),
        _const_spec(w['conv_w'].shape, layer), _const_spec(w['conv_b'].shape, layer),
        _const_spec(w['w_ai'].shape, layer), _const_spec(w['b_a'].shape, layer),
        _const_spec(w['b_i'].shape, layer), _const_spec(w['lru_lambda'].shape, layer),
        _const_spec(w['w_br_lru'].shape, layer),
    ]
    dq = N_STREAMS * LANES
    dk = N_KV_HEADS * 2 * D_HEAD
    dv = N_KV_HEADS * D_VHEAD
    head_spec = lambda nh, c: pl.BlockSpec((None, nh, rows, c), lambda bi, i: (bi, 0, i, 0))
    out_shape = [
        jax.ShapeDtypeStruct((b, N_STREAMS, r, LANES) if is_prompt else (b, r, dq), _BF16),
        jax.ShapeDtypeStruct((b, r, dk), _F32),
        jax.ShapeDtypeStruct((b, r, dv), _F32), jax.ShapeDtypeStruct((b, r, d), _F32),
        jax.ShapeDtypeStruct((b, r, d), _F32), jax.ShapeDtypeStruct((b, hist, d), _F32),
        jax.ShapeDtypeStruct((b, stride, d), _F32),
    ]
    out_specs = [head_spec(N_STREAMS, LANES) if is_prompt else row_spec(dq),
                 row_spec(dk), row_spec(dv), row_spec(d), row_spec(d),
                 state_spec(hist), state_spec(stride)]
    if is_prompt:
        out_shape += [jax.ShapeDtypeStruct((b, 2 * N_KV_HEADS, r, LANES), _BF16),
                      jax.ShapeDtypeStruct((b, N_KV_HEADS, r, 2 * D_VHEAD), _BF16)]
        out_specs += [head_spec(2 * N_KV_HEADS, LANES), head_spec(N_KV_HEADS, 2 * D_VHEAD)]
    kern = functools.partial(_pre_kernel, stride=stride, rows=rows, pad=pad, is_prompt=is_prompt)
    return pl.pallas_call(
        kern, grid=grid, in_specs=in_specs, out_specs=out_specs, out_shape=out_shape,
        scratch_shapes=[pltpu.VMEM((pad + rows, d), _F32), pltpu.VMEM((rows, d), _F32),
                        pltpu.VMEM((rows, d), _F32), pltpu.VMEM((stride, d), _F32)],
        compiler_params=pltpu.CompilerParams(dimension_semantics=("arbitrary", "arbitrary"),
                                             vmem_limit_bytes=VMEM_LIMIT_BYTES),
        name="pre_prompt" if is_prompt else "pre_sample",
    )(x, conv_state, h_state, w['g_mix'], w['w_in'], w['conv_w'], w['conv_b'], w['w_ai'], w['b_a'],
      w['b_i'], w['lru_lambda'], w['w_br_lru'])


def _attn_prompt_kernel(qt_ref, kt_ref, q_ref, k_ref, v_ref, lq1_ref, lk1_ref, lq2_ref, lk2_ref, gs_ref,
                        o_ref, m_scr, acc_scr, *, lam_init, tq, tk):
    s_idx = pl.program_id(1)
    qi = qt_ref[s_idx]
    kj = kt_ref[s_idx]
    rows = GROUP * tq
    n_pairs = 2 * N_KV_HEADS

    @pl.when(kj == 0)
    def _():
        m_scr[...] = jnp.full(m_scr.shape, NEG_BIG, _F32)
        acc_scr[...] = jnp.zeros(acc_scr.shape, _F32)

    def update(masked):
        if masked:
            row = lax.broadcasted_iota(jnp.int32, (rows, tk), 0) % tq
            col = lax.broadcasted_iota(jnp.int32, (rows, tk), 1)
            keep = (col - row) <= (qi * tq - kj * tk)
        for pair in range(n_pairs):
            q = q_ref[GROUP * pair:GROUP * (pair + 1)].reshape(rows, LANES)
            s = _dot_nt(q, k_ref[pair])
            if masked:
                s = jnp.where(keep, s, NEG_BIG)
            m_prev = m_scr[pair]
            m_new = jnp.maximum(m_prev, jnp.max(s, axis=1, keepdims=True))
            alpha = jnp.exp(m_prev - m_new)
            p = jnp.exp(s - jnp.concatenate([m_new] * (tk // LANES), axis=1))
            pv = _dot(p.astype(_BF16), v_ref[pair // 2])
            acc_scr[pair] = jnp.concatenate([alpha, alpha], axis=1) * acc_scr[pair] + pv
            m_scr[pair] = m_new

    is_last = kj == (qi * tq) // tk

    @pl.when(jnp.logical_not(is_last))
    def _():
        update(False)

    @pl.when(is_last)
    def _():
        update(True)
        lam = _lam_value(lq1_ref[...], lk1_ref[...], lq2_ref[...], lk2_ref[...], lam_init)
        for hg in range(N_HEADS):
            h, g = divmod(hg, GROUP)
            a0 = acc_scr[2 * h, g * tq:(g + 1) * tq, :]
            a1 = acc_scr[2 * h + 1, g * tq:(g + 1) * tq, :]
            o = a0[:, :D_VHEAD] / a0[:, D_VHEAD:] - lam * (a1[:, :D_VHEAD] / a1[:, D_VHEAD:])
            o_ref[:, hg * LANES:(hg + 1) * LANES] = (_rms(o, gs_ref[...]) * (1.0 - lam_init)).astype(o_ref.dtype)


def _attn_prompt_call(q, k, v, w, layer):
    b, _, t, _ = q.shape
    tq = min(ATTN_Q_BLOCK, t)
    tk = min(ATTN_K_BLOCK, t)
    assert t % tk == 0 and tk % tq == 0 and t <= POS_SPLIT * 256
    pairs = [(qi, kj) for qi in range(t // tq) for kj in range((qi * tq) // tk + 1)]
    qt = jnp.asarray([p[0] for p in pairs], jnp.int32)
    kt = jnp.asarray([p[1] for p in pairs], jnp.int32)
    small = lambda a: pl.BlockSpec((None,) + tuple(a.shape[1:]), lambda bi, s, qt, kt: (layer, 0, 0))
    grid_spec = pltpu.PrefetchScalarGridSpec(
        num_scalar_prefetch=2, grid=(b, len(pairs)),
        in_specs=[
            pl.BlockSpec((None, q.shape[1], tq, LANES), lambda bi, s, qt, kt: (bi, 0, qt[s], 0)),
            pl.BlockSpec((None, k.shape[1], tk, LANES), lambda bi, s, qt, kt: (bi, 0, kt[s], 0)),
            pl.BlockSpec((None, v.shape[1], tk, v.shape[-1]), lambda bi, s, qt, kt: (bi, 0, kt[s], 0)),
            small(w['lam_q1']), small(w['lam_k1']), small(w['lam_q2']), small(w['lam_k2']),
            small(w['g_subln']),
        ],
        out_specs=pl.BlockSpec((None, tq, N_HEADS * D_VHEAD), lambda bi, s, qt, kt: (bi, qt[s], 0)),
        scratch_shapes=[pltpu.VMEM((2 * N_KV_HEADS, GROUP * tq, LANES), _F32),
                        pltpu.VMEM((2 * N_KV_HEADS, GROUP * tq, 2 * D_VHEAD), _F32)],
    )
    kern = functools.partial(_attn_prompt_kernel, lam_init=_lam_init(layer), tq=tq, tk=tk)
    return pl.pallas_call(
        kern, grid_spec=grid_spec,
        out_shape=jax.ShapeDtypeStruct((b, t, N_HEADS * D_VHEAD), _BF16),
        compiler_params=pltpu.CompilerParams(dimension_semantics=("arbitrary", "arbitrary"),
                                             vmem_limit_bytes=VMEM_LIMIT_BYTES),
        name="attn_prompt",
    )(qt, kt, q, k, v, w['lam_q1'], w['lam_k1'], w['lam_q2'], w['lam_k2'], w['g_subln'])


def _attn_sample_kernel(pt_ref, q_ref, kn_ref, vn_ref, lq1_ref, lk1_ref, lq2_ref, lk2_ref, gs_ref, *rest,
                        lam_init, n_pages, page, n_new):
    k_pages = rest[:n_pages]
    v_pages = rest[n_pages:2 * n_pages]
    o_ref = rest[2 * n_pages]
    del pt_ref
    past = n_pages * page
    n_rows = q_ref.shape[0]
    rows_per_head = GROUP * 2 * n_new
    r_idx = lax.broadcasted_iota(jnp.int32, (n_rows, 1), 0)
    h_row = r_idx // rows_per_head
    hg_row = r_idx // (2 * n_new)
    t_row = r_idx % n_new
    slope = jnp.zeros((n_rows, 1), _F32)
    for hg in range(N_HEADS):
        slope = jnp.where(hg_row == hg, SLOPES[hg], slope)
    q_pos = (past + t_row).astype(_F32)

    width = page * N_KV_HEADS
    col = lax.broadcasted_iota(jnp.int32, (1, width), 1)
    valid = (col % N_KV_HEADS) == h_row
    base = jnp.where(valid, slope * ((col // N_KV_HEADS).astype(_F32) - q_pos), NEG_BIG)
    q = q_ref[...]
    s_pages = [_dot_nt(q, kp[...].astype(_BF16)) + (base + slope * float(j * page))
               for j, kp in enumerate(k_pages)]

    n_cols_new = kn_ref.shape[0]
    col_n = lax.broadcasted_iota(jnp.int32, (1, n_cols_new), 1)
    t_key = col_n // N_KV_HEADS
    valid_n = ((col_n % N_KV_HEADS) == h_row) & (t_key <= t_row)
    s_new = _dot_nt(q, kn_ref[...].astype(_BF16))
    s_new = jnp.where(valid_n, s_new - slope * (t_row - t_key).astype(_F32), NEG_BIG)

    m = jnp.max(s_new, axis=-1, keepdims=True)
    for s in s_pages:
        m = jnp.maximum(m, jnp.max(s, axis=-1, keepdims=True))
    p_new = jnp.exp(s_new - m)
    l = jnp.sum(p_new, axis=-1, keepdims=True)
    acc = _dot(p_new.astype(_BF16), vn_ref[...].astype(_BF16))
    for s, vp in zip(s_pages, v_pages):
        p = jnp.exp(s - m)
        l = l + jnp.sum(p, axis=-1, keepdims=True)
        acc = acc + _dot(p.astype(_BF16), vp[...].astype(_BF16))
    on = acc / l
    lam = _lam_value(lq1_ref[...], lk1_ref[...], lq2_ref[...], lk2_ref[...], lam_init)
    o = on - lam * pltpu.roll(on, n_rows - n_new, 0)
    o_ref[...] = (_rms(o, gs_ref[...]) * (1.0 - lam_init)).astype(o_ref.dtype)


def _attn_sample_call(q, k_new, v_new, cache_k, cache_v, page_table, w, layer):
    n, n_rows, _ = q.shape
    n_new = n_rows // (N_KV_HEADS * GROUP * 2)
    n_pages = page_table.shape[1]
    page = cache_k.shape[2] // N_KV_HEADS
    small = lambda a: pl.BlockSpec((None,) + tuple(a.shape[1:]), lambda bi, pt: (layer, 0, 0))
    per_seq = lambda a: pl.BlockSpec((None,) + tuple(a.shape[1:]), lambda bi, pt: (bi, 0, 0))
    page_spec = lambda j: pl.BlockSpec((None, None) + tuple(cache_k.shape[2:]),
                                       lambda bi, pt: (layer, pt[bi * n_pages + j], 0, 0))
    grid_spec = pltpu.PrefetchScalarGridSpec(
        num_scalar_prefetch=1, grid=(n,),
        in_specs=[per_seq(q), per_seq(k_new), per_seq(v_new),
                  small(w['lam_q1']), small(w['lam_k1']), small(w['lam_q2']), small(w['lam_k2']),
                  small(w['g_subln'])]
                 + [page_spec(j) for j in range(n_pages)] + [page_spec(j) for j in range(n_pages)],
        out_specs=pl.BlockSpec((None, n_rows, LANES), lambda bi, pt: (bi, 0, 0)),
    )
    kern = functools.partial(_attn_sample_kernel, lam_init=_lam_init(layer), n_pages=n_pages, page=page,
                             n_new=n_new)
    return pl.pallas_call(
        kern, grid_spec=grid_spec,
        out_shape=jax.ShapeDtypeStruct((n, n_rows, LANES), _BF16),
        compiler_params=pltpu.CompilerParams(dimension_semantics=("arbitrary",),
                                             vmem_limit_bytes=VMEM_LIMIT_BYTES),
        name="attn_sample",
    )(page_table.reshape(-1), q, k_new, v_new, w['lam_q1'], w['lam_k1'], w['lam_q2'], w['lam_k2'],
      w['g_subln'], *([cache_k] * n_pages), *([cache_v] * n_pages))


def _post_kernel(h_ref, m_ref, g_ref, o_ref, p_ref, wba_ref, wo_ref, gffn_ref, wg_ref, wu_ref, wd_ref,
                 gple_ref, wpg_ref, wp_ref, gfin_ref, out_ref, *, is_last):
    mix = m_ref[...] + g_ref[...] * _dot(o_ref[...], wba_ref[...])
    h = h_ref[...] + _dot(mix.astype(_BF16), wo_ref[...])
    n2 = _rms(h, gffn_ref[...]).astype(_BF16)
    act = jax.nn.silu(_dot(n2, wg_ref[...])) * _dot(n2, wu_ref[...])
    h = h + _dot(act.astype(_BF16), wd_ref[...])
    n3 = _rms(h, gple_ref[...]).astype(_BF16)
    pg = jax.nn.sigmoid(_dot(n3, wpg_ref[...]))
    h = h + _dot(p_ref[...].astype(_BF16), wp_ref[...]) * pg
    out_ref[...] = _rms(h, gfin_ref[...]) if is_last else h


def _post_call(h, m_lru, g_att, o, p, w, layer, *, is_last):
    r, d = h.shape
    rows = min(ROW_BLOCK, r)
    assert r % rows == 0
    row_spec = lambda c: pl.BlockSpec((rows, c), lambda i: (i, 0))
    names = ['w_br_att', 'w_o', 'g_ffn', 'w_gate', 'w_up', 'w_down', 'g_ple', 'w_ple_gate', 'w_ple']
    kern = functools.partial(_post_kernel, is_last=is_last)
    return pl.pallas_call(
        kern, grid=(r // rows,),
        in_specs=[row_spec(d), row_spec(d), row_spec(d), row_spec(o.shape[-1]),
                  pl.BlockSpec((None, rows, p.shape[-1]), lambda i: (layer, i, 0))]
                 + [_const_spec(w[nm].shape, layer) for nm in names]
                 + [pl.BlockSpec(w['g_final'].shape, lambda i: (0, 0), pipeline_mode=pl.Buffered(1))],
        out_specs=row_spec(d),
        out_shape=jax.ShapeDtypeStruct((r, d), _F32),
        compiler_params=pltpu.CompilerParams(dimension_semantics=("arbitrary",),
                                             vmem_limit_bytes=VMEM_LIMIT_BYTES),
        name="post",
    )(h, m_lru, g_att, o, p, *[w[nm] for nm in names], w['g_final'])


def kernel(x_prompt, x_sample, p_prompt, p_sample, cache_k, cache_v, page_table, state_conv, state_h,
           w_in, g_mix, conv_w, conv_b, w_a, b_a, w_i, b_i, lru_lambda, lam_q1, lam_k1, lam_q2, lam_k2,
           g_subln, w_br_lru, w_br_att, w_o, g_ffn, w_gate, w_up, w_down, g_ple, w_ple_gate, w_ple, g_final):
    depth, d = g_mix.shape
    bp, t, _ = x_prompt.shape
    ns, ts, _ = x_sample.shape
    hist = CONV_W - 1
    row3 = lambda a: a.reshape(depth, 1, a.shape[-1])
    w = dict(
        w_in=w_in.astype(_BF16), g_mix=row3(g_mix), conv_w=conv_w, conv_b=row3(conv_b),
        w_ai=jnp.concatenate([w_a, w_i], axis=-1).astype(_BF16), b_a=row3(b_a), b_i=row3(b_i),
        lru_lambda=row3(lru_lambda), w_br_lru=w_br_lru.astype(_BF16),
        lam_q1=row3(lam_q1), lam_k1=row3(lam_k1), lam_q2=row3(lam_q2), lam_k2=row3(lam_k2),
        g_subln=row3(g_subln), w_br_att=w_br_att.astype(_BF16), w_o=w_o.astype(_BF16), g_ffn=row3(g_ffn),
        w_gate=w_gate.astype(_BF16), w_up=w_up.astype(_BF16), w_down=w_down.astype(_BF16),
        g_ple=row3(g_ple), w_ple_gate=w_ple_gate.astype(_BF16), w_ple=w_ple.astype(_BF16),
        g_final=g_final.reshape(1, d),
    )
    ck = cache_k.reshape(cache_k.shape[0], cache_k.shape[1], -1, cache_k.shape[-1])
    cv = cache_v.reshape(cache_v.shape[0], cache_v.shape[1], -1, cache_v.shape[-1])
    dk = N_KV_HEADS * 2 * D_HEAD
    dv = N_KV_HEADS * D_VHEAD

    tm = lambda a: jnp.swapaxes(a, -3, -2)
    hp = x_prompt
    hs = tm(x_sample).reshape(1, ts * ns, d)
    cs_all = tm(state_conv).reshape(depth, 1, hist * ns, d)
    ps_all = tm(p_sample).reshape(depth, ts * ns, -1)
    zero_c = jnp.zeros((bp, hist, d), _F32)
    zero_h = jnp.zeros((bp, 1, d), _F32)
    rows_p = min(ROW_BLOCK, t)
    pp_all = p_prompt.reshape(depth, bp * t, -1)

    ks_p, vs_p, cs_p, hs_p, ks_s, vs_s, cs_s, hs_s = ([] for _ in range(8))
    for l in range(depth):
        last = l == depth - 1
        q, k, v, m_lru, g_att, c_out, h_out, k_rdy, v_aug = _pre_call(
            hp, zero_c, zero_h, w, l, stride=1, rows=rows_p, is_prompt=True)
        o = _attn_prompt_call(q, k_rdy, v_aug, w, l)
        hp = _post_call(hp.reshape(bp * t, d), m_lru.reshape(bp * t, d), g_att.reshape(bp * t, d),
                        o.reshape(bp * t, -1), pp_all, w, l,
                        is_last=last).reshape(bp, t, d)
        ks_p.append(k.reshape(bp, t, N_KV_HEADS, 2 * D_HEAD))
        vs_p.append(v.reshape(bp, t, N_KV_HEADS, D_VHEAD))
        cs_p.append(c_out)
        hs_p.append(h_out.reshape(bp, d))
        q, k, v, m_lru, g_att, c_out, h_out = _pre_call(
            hs, cs_all[l], state_h[l][None], w, l, stride=ns, rows=ts * ns, is_prompt=False)
        qs = q.reshape(ts, ns, N_KV_HEADS, GROUP, 2, LANES).transpose(1, 2, 3, 4, 0, 5)
        qs = qs.reshape(ns, N_STREAMS * ts, LANES)
        k_sm = tm(k.reshape(ts, ns, dk))
        v_sm = tm(v.reshape(ts, ns, dv))
        o = _attn_sample_call(qs, k_sm.reshape(ns, ts * N_KV_HEADS, -1), v_sm.reshape(ns, ts * N_KV_HEADS, -1),
                              ck, cv, page_table, w, l)
        o = o.reshape(ns, N_KV_HEADS, GROUP, 2, ts, D_VHEAD)[:, :, :, 0]
        o = o.transpose(3, 0, 1, 2, 4).reshape(ts * ns, -1)
        hs = _post_call(hs.reshape(ts * ns, d), m_lru.reshape(ts * ns, d), g_att.reshape(ts * ns, d),
                        o, ps_all, w, l, is_last=last).reshape(1, ts * ns, d)
        ks_s.append(k_sm.reshape(ns, ts, N_KV_HEADS, 2 * D_HEAD))
        vs_s.append(v_sm.reshape(ns, ts, N_KV_HEADS, D_VHEAD))
        cs_s.append(tm(c_out.reshape(hist, ns, d)))
        hs_s.append(h_out.reshape(ns, d))

    y_sample = tm(hs.reshape(ts, ns, d))
    return (hp, y_sample, jnp.stack(ks_p), jnp.stack(vs_p), jnp.stack(cs_p), jnp.stack(hs_p),
            jnp.stack(ks_s), jnp.stack(vs_s), jnp.stack(cs_s), jnp.stack(hs_s))
```

```python
import functools
import math

import jax
import jax.numpy as jnp
from jax import lax
from jax.experimental import pallas as pl
from jax.experimental.pallas import tpu as pltpu

_F32 = jnp.float32
_BF16 = jnp.bfloat16

N_HEADS = 8
N_KV_HEADS = 4
GROUP = N_HEADS // N_KV_HEADS
D_HEAD = 64
D_VHEAD = 2 * D_HEAD
N_STREAMS = N_HEADS * 2
N_LRU_BLOCKS = 8
CONV_W = 4
LRU_C = 8.0
EPS = 1e-6
SLOPES = tuple(2.0 ** (-8.0 * (i + 1) / N_HEADS) for i in range(N_HEADS))
NEG_BIG = -1e30

LANES = 128
SUBLANES = 8
VMEM_LIMIT_BYTES = 56 * 1024 * 1024

ROW_BLOCK = 256
ATTN_Q_BLOCK = 256
ATTN_K_BLOCK = 1024
POS_SPLIT = 64


def _lam_init(layer):
    return 0.8 - 0.6 * math.exp(-0.3 * layer)


def _rms(x, g):
    return x * lax.rsqrt(jnp.mean(x * x, axis=-1, keepdims=True) + EPS) * g


def _dot(a, b):
    return jnp.dot(a, b, preferred_element_type=_F32)


def _dot_nt(a, b):
    return lax.dot_general(a, b, (((1,), (1,)), ((), ())), preferred_element_type=_F32)


def _softplus(x):
    return jnp.maximum(x, 0.0) + jnp.log1p(jnp.exp(-jnp.abs(x)))


def _lam_value(q1, k1, q2, k2, lam_init):
    s1 = jnp.sum(q1 * k1, axis=-1, keepdims=True)
    s2 = jnp.sum(q2 * k2, axis=-1, keepdims=True)
    return jnp.exp(s1) - jnp.exp(s2) + lam_init


def _pre_kernel(x_ref, cs_ref, hs_ref, gmix_ref, w_in_ref, cw_ref, cb_ref, wai_ref, ba_ref, bi_ref,
                lam_ref, wbr_ref, *rest, stride, rows, pad, is_prompt, n_alias):
    rest = rest[n_alias:]
    if is_prompt:
        (q_ref, k_ref, v_ref, m_ref, g_ref, co_ref, ho_ref, kr_ref, va_ref,
         up_scr, a_scr, x_scr, gelu_scr, h_scr) = rest
    else:
        (q_ref, k_ref, v_ref, m_ref, g_ref, co_ref, ho_ref,
         up_scr, a_scr, x_scr, gelu_scr, h_scr) = rest
    d = x_ref.shape[-1]
    i = pl.program_id(1)
    hist = (CONV_W - 1) * stride

    n = _rms(x_ref[...], gmix_ref[...]).astype(_BF16)

    @pl.when(i == 0)
    def _():
        up_scr[pl.ds(pad - hist, hist), :] = cs_ref[...]
        h_scr[...] = hs_ref[...]

    up_scr[pl.ds(pad, rows), :] = _dot(n, w_in_ref[:, 0:d])
    uc = up_scr[pl.ds(pad - hist, rows), :] * cw_ref[0:1, :]
    for j in range(1, CONV_W):
        uc = uc + up_scr[pl.ds(pad - hist + j * stride, rows), :] * cw_ref[j:j + 1, :]
    uc = uc + cb_ref[...]
    tail = up_scr[pl.ds(pad + rows - hist, hist), :]
    co_ref[...] = tail
    up_scr[pl.ds(pad - hist, hist), :] = tail

    ucb = uc.astype(_BF16)
    blk = d // N_LRU_BLOCKS
    for nb in range(N_LRU_BLOCKS):
        sl = slice(nb * blk, (nb + 1) * blk)
        gates = _dot(ucb[:, sl], wai_ref[nb])
        r = jax.nn.sigmoid(gates[:, :blk] + ba_ref[:, sl])
        ig = jax.nn.sigmoid(gates[:, blk:] + bi_ref[:, sl])
        a = jnp.exp(-LRU_C * r * _softplus(-lam_ref[:, sl]))
        a_scr[:, sl] = a
        x_scr[:, sl] = jnp.sqrt(1.0 - a * a) * (ig * uc[:, sl])

    gelu_scr[...] = jax.nn.gelu(_dot(n, w_in_ref[:, d:2 * d]))

    o_q = 2 * d
    o_k = o_q + N_HEADS * 2 * D_HEAD
    o_v = o_k + N_KV_HEADS * 2 * D_HEAD
    o_g = o_v + N_KV_HEADS * D_VHEAD
    lane = lax.broadcasted_iota(jnp.int32, (1, LANES), 1)
    low = lane < D_HEAD
    qz = _dot(n, w_in_ref[:, o_q:o_k]) * (D_HEAD ** -0.5)
    kz = _dot(n, w_in_ref[:, o_k:o_v])
    vz = _dot(n, w_in_ref[:, o_v:o_g])
    if is_prompt:
        for h in range(N_KV_HEADS):
            k_ref[pl.ds(h, rows, stride=N_KV_HEADS), :] = kz[:, h * LANES:(h + 1) * LANES]
            v_ref[pl.ds(h, rows, stride=N_KV_HEADS), :] = vz[:, h * LANES:(h + 1) * LANES]
    else:
        k_ref[...] = kz
        v_ref[...] = vz
    if is_prompt:
        pos = i * rows + lax.broadcasted_iota(jnp.int32, (rows, 1), 0)
        hi = (pos // POS_SPLIT).astype(_F32)
        lo = (pos % POS_SPLIT).astype(_F32)
        k_extra = jnp.where(lane == D_HEAD, hi, jnp.where(lane == D_HEAD + 1, lo, 0.0))
        ones = jnp.ones((rows, LANES), _BF16)
        for h in range(N_KV_HEADS):
            chunk = kz[:, h * LANES:(h + 1) * LANES]
            kr_ref[2 * h] = jnp.where(low, chunk, k_extra).astype(_BF16)
            kr_ref[2 * h + 1] = jnp.where(low, pltpu.roll(chunk, D_HEAD, 1), k_extra).astype(_BF16)
            va_ref[h, :, 0:D_VHEAD] = vz[:, h * LANES:(h + 1) * LANES].astype(_BF16)
            va_ref[h, :, D_VHEAD:2 * D_VHEAD] = ones
    for hg in range(N_HEADS):
        chunk = qz[:, hg * LANES:(hg + 1) * LANES]
        if is_prompt:
            h, g = divmod(hg, GROUP)
            q_extra = jnp.where(lane == D_HEAD, SLOPES[hg] * POS_SPLIT,
                                jnp.where(lane == D_HEAD + 1, SLOPES[hg], 0.0))
            q_ref[(2 * h) * GROUP + g] = jnp.where(low, chunk, q_extra).astype(_BF16)
            q_ref[(2 * h + 1) * GROUP + g] = jnp.where(low, pltpu.roll(chunk, D_HEAD, 1), q_extra).astype(_BF16)
        else:
            q_ref[:, (2 * hg) * LANES:(2 * hg + 1) * LANES] = jnp.where(low, chunk, 0.0).astype(_BF16)
            q_ref[:, (2 * hg + 1) * LANES:(2 * hg + 2) * LANES] = jnp.where(low, 0.0, chunk).astype(_BF16)

    m_ref[...] = jax.nn.sigmoid(_dot(n, w_in_ref[:, o_g:o_g + d]))
    g_ref[...] = jax.nn.sigmoid(_dot(n, w_in_ref[:, o_g + d:o_g + 2 * d]))

    n_steps = rows // stride
    if stride == 1 and n_steps % 16 == 0:
        half = n_steps // 2

        def step2(t, carry):
            h1, x2, a2 = carry
            r1 = pl.ds(t, 1)
            r2 = pl.ds(half + t, 1)
            h1 = a_scr[r1, :] * h1 + x_scr[r1, :]
            a_row = a_scr[r2, :]
            x2 = a_row * x2 + x_scr[r2, :]
            a2 = a_row * a2
            x_scr[r1, :] = h1
            x_scr[r2, :] = x2
            a_scr[r2, :] = a2
            return h1, x2, a2

        h_mid, x2, a2 = lax.fori_loop(0, half, step2,
                                      (h_scr[...], jnp.zeros((1, d), _F32), jnp.ones((1, d), _F32)), unroll=8)
        x_scr[half:, :] = x_scr[half:, :] + a_scr[half:, :] * h_mid
        h_last = x2 + a2 * h_mid
    else:
        def step(t, h):
            sl = pl.ds(pl.multiple_of(t * stride, stride), stride)
            h = a_scr[sl, :] * h + x_scr[sl, :]
            x_scr[sl, :] = h
            return h

        h_last = lax.fori_loop(0, n_steps, step, h_scr[...], unroll=min(8, n_steps))
    h_scr[...] = h_last
    ho_ref[...] = h_last

    b_lru = _dot((x_scr[...] * gelu_scr[...]).astype(_BF16), wbr_ref[...])
    m_ref[...] = m_ref[...] * b_lru


def _const_spec(shape, layer):
    nd = len(shape)
    return pl.BlockSpec((None,) + tuple(shape[1:]), lambda *_: (layer,) + (0,) * (nd - 1),
                        pipeline_mode=pl.Buffered(1))


def _pre_call(x, conv_state, h_state, w, layer, *, stride, rows, is_prompt, kv_stack=None):
    b, r, d = x.shape
    assert r % rows == 0 and rows % stride == 0 and rows >= (CONV_W - 1) * stride
    hist = (CONV_W - 1) * stride
    pad = -(-hist // SUBLANES) * SUBLANES
    grid = (b, r // rows)
    depth = w['g_mix'].shape[0]
    row_spec = lambda c: pl.BlockSpec((None, rows, c), lambda bi, i: (bi, i, 0))
    state_spec = lambda c: pl.BlockSpec((None, c, d), lambda bi, i: (bi, 0, 0))
    in_specs = [
        row_spec(d), state_spec(hist), state_spec(stride),
        _const_spec(w['g_mix'].shape, layer), _const_spec(w['w_in'].shape, layer),
        _const_spec(w['conv_w'].shape, layer), _const_spec(w['conv_b'].shape, layer),
        _const_spec(w['w_ai'].shape, layer), _const_spec(w['b_a'].shape, layer),
        _const_spec(w['b_i'].shape, layer), _const_spec(w['lru_lambda'].shape, layer),
        _const_spec(w['w_br_lru'].shape, layer),
    ]
    args = [x, conv_state, h_state, w['g_mix'], w['w_in'], w['conv_w'], w['conv_b'], w['w_ai'], w['b_a'],
            w['b_i'], w['lru_lambda'], w['w_br_lru']]
    aliases = {}
    if kv_stack is not None:
        aliases = {len(args): 1, len(args) + 1: 2}
        args += list(kv_stack)
        in_specs += [pl.BlockSpec(memory_space=pl.ANY)] * 2
    dq = N_STREAMS * LANES
    dk = N_KV_HEADS * 2 * D_HEAD
    dv = N_KV_HEADS * D_VHEAD
    head_spec = lambda nh, c: pl.BlockSpec((None, nh, rows, c), lambda bi, i: (bi, 0, i, 0))
    if is_prompt:
        kv_shape = jax.ShapeDtypeStruct((depth, b, r * N_KV_HEADS, LANES), _F32)
        kv_spec = pl.BlockSpec((None, None, rows * N_KV_HEADS, LANES), lambda bi, i: (layer, bi, i, 0))
        out_shape = [jax.ShapeDtypeStruct((b, N_STREAMS, r, LANES), _BF16), kv_shape, kv_shape]
        out_specs = [head_spec(N_STREAMS, LANES), kv_spec, kv_spec]
    else:
        out_shape = [jax.ShapeDtypeStruct((b, r, dq), _BF16), jax.ShapeDtypeStruct((b, r, dk), _F32),
                     jax.ShapeDtypeStruct((b, r, dv), _F32)]
        out_specs = [row_spec(dq), row_spec(dk), row_spec(dv)]
    out_shape += [jax.ShapeDtypeStruct((b, r, d), _F32), jax.ShapeDtypeStruct((b, r, d), _F32),
                  jax.ShapeDtypeStruct((b, hist, d), _F32), jax.ShapeDtypeStruct((b, stride, d), _F32)]
    out_specs += [row_spec(d), row_spec(d), state_spec(hist), state_spec(stride)]
    if is_prompt:
        out_shape += [jax.ShapeDtypeStruct((b, 2 * N_KV_HEADS, r, LANES), _BF16),
                      jax.ShapeDtypeStruct((b, N_KV_HEADS, r, 2 * D_VHEAD), _BF16)]
        out_specs += [head_spec(2 * N_KV_HEADS, LANES), head_spec(N_KV_HEADS, 2 * D_VHEAD)]
    kern = functools.partial(_pre_kernel, stride=stride, rows=rows, pad=pad, is_prompt=is_prompt,
                             n_alias=len(aliases))
    return pl.pallas_call(
        kern, grid=grid, in_specs=in_specs, out_specs=out_specs, out_shape=out_shape,
        input_output_aliases=aliases,
        scratch_shapes=[pltpu.VMEM((pad + rows, d), _F32), pltpu.VMEM((rows, d), _F32),
                        pltpu.VMEM((rows, d), _F32), pltpu.VMEM((rows, d), _F32),
                        pltpu.VMEM((stride, d), _F32)],
        compiler_params=pltpu.CompilerParams(dimension_semantics=("arbitrary", "arbitrary"),
                                             vmem_limit_bytes=VMEM_LIMIT_BYTES),
        name="pre_prompt" if is_prompt else "pre_sample",
    )(*args)


def _attn_prompt_kernel(qt_ref, kt_ref, q_ref, k_ref, v_ref, lq1_ref, lk1_ref, lq2_ref, lk2_ref, gs_ref,
                        o_ref, m_scr, acc_scr, *, lam_init, tq, tk):
    s_idx = pl.program_id(1)
    qi = qt_ref[s_idx]
    kj = kt_ref[s_idx]
    rows = GROUP * tq
    n_pairs = 2 * N_KV_HEADS

    @pl.when(kj == 0)
    def _():
        m_scr[...] = jnp.full(m_scr.shape, NEG_BIG, _F32)
        acc_scr[...] = jnp.zeros(acc_scr.shape, _F32)

    def update(n_cols, masked):
        if masked:
            row = lax.broadcasted_iota(jnp.int32, (rows, tq), 0) % tq
            col = lax.broadcasted_iota(jnp.int32, (rows, tq), 1)
            keep = col <= row
        for pair in range(n_pairs):
            q = q_ref[GROUP * pair:GROUP * (pair + 1)].reshape(rows, LANES)
            s = _dot_nt(q, k_ref[pair, 0:n_cols, :])
            if masked:
                diag = jnp.where(keep, s[:, n_cols - tq:], NEG_BIG)
                s = diag if n_cols == tq else jnp.concatenate([s[:, :n_cols - tq], diag], axis=1)
            m_prev = m_scr[pair]
            m_new = jnp.maximum(m_prev, jnp.max(s, axis=1, keepdims=True))
            alpha = jnp.exp(m_prev - m_new)
            p = jnp.exp(s - jnp.concatenate([m_new] * (n_cols // LANES), axis=1))
            pv = _dot(p.astype(_BF16), v_ref[pair // 2, 0:n_cols, :])
            acc_scr[pair] = jnp.concatenate([alpha, alpha], axis=1) * acc_scr[pair] + pv
            m_scr[pair] = m_new

    n_sub = tk // tq
    is_last = kj == qi // n_sub

    @pl.when(jnp.logical_not(is_last))
    def _():
        update(tk, False)

    for r in range(n_sub):
        @pl.when(jnp.logical_and(is_last, qi % n_sub == r))
        def _():
            update((r + 1) * tq, True)

    @pl.when(is_last)
    def _():
        lam = _lam_value(lq1_ref[...], lk1_ref[...], lq2_ref[...], lk2_ref[...], lam_init)
        for hg in range(N_HEADS):
            h, g = divmod(hg, GROUP)
            a0 = acc_scr[2 * h, g * tq:(g + 1) * tq, :]
            a1 = acc_scr[2 * h + 1, g * tq:(g + 1) * tq, :]
            o = a0[:, :D_VHEAD] / a0[:, D_VHEAD:] - lam * (a1[:, :D_VHEAD] / a1[:, D_VHEAD:])
            o_ref[:, hg * LANES:(hg + 1) * LANES] = (_rms(o, gs_ref[...]) * (1.0 - lam_init)).astype(o_ref.dtype)


def _attn_prompt_call(q, k, v, w, layer):
    b, _, t, _ = q.shape
    tq = min(ATTN_Q_BLOCK, t)
    tk = min(ATTN_K_BLOCK, t)
    assert t % tk == 0 and tk % tq == 0 and t <= POS_SPLIT * 256
    pairs = [(qi, kj) for qi in range(t // tq) for kj in range((qi * tq) // tk + 1)]
    qt = jnp.asarray([p[0] for p in pairs], jnp.int32)
    kt = jnp.asarray([p[1] for p in pairs], jnp.int32)
    small = lambda a: pl.BlockSpec((None,) + tuple(a.shape[1:]), lambda bi, s, qt, kt: (layer, 0, 0))
    grid_spec = pltpu.PrefetchScalarGridSpec(
        num_scalar_prefetch=2, grid=(b, len(pairs)),
        in_specs=[
            pl.BlockSpec((None, q.shape[1], tq, LANES), lambda bi, s, qt, kt: (bi, 0, qt[s], 0)),
            pl.BlockSpec((None, k.shape[1], tk, LANES), lambda bi, s, qt, kt: (bi, 0, kt[s], 0)),
            pl.BlockSpec((None, v.shape[1], tk, v.shape[-1]), lambda bi, s, qt, kt: (bi, 0, kt[s], 0)),
            small(w['lam_q1']), small(w['lam_k1']), small(w['lam_q2']), small(w['lam_k2']),
            small(w['g_subln']),
        ],
        out_specs=pl.BlockSpec((None, tq, N_HEADS * D_VHEAD), lambda bi, s, qt, kt: (bi, qt[s], 0)),
        scratch_shapes=[pltpu.VMEM((2 * N_KV_HEADS, GROUP * tq, LANES), _F32),
                        pltpu.VMEM((2 * N_KV_HEADS, GROUP * tq, 2 * D_VHEAD), _F32)],
    )
    kern = functools.partial(_attn_prompt_kernel, lam_init=_lam_init(layer), tq=tq, tk=tk)
    return pl.pallas_call(
        kern, grid_spec=grid_spec,
        out_shape=jax.ShapeDtypeStruct((b, t, N_HEADS * D_VHEAD), _BF16),
        compiler_params=pltpu.CompilerParams(dimension_semantics=("arbitrary", "arbitrary"),
                                             vmem_limit_bytes=VMEM_LIMIT_BYTES),
        name="attn_prompt",
    )(qt, kt, q, k, v, w['lam_q1'], w['lam_k1'], w['lam_q2'], w['lam_k2'], w['g_subln'])


def _attn_sample_kernel(pt_ref, q_ref, kn_ref, vn_ref, lq1_ref, lk1_ref, lq2_ref, lk2_ref, gs_ref, *rest,
                        lam_init, n_pages, page, n_new):
    k_pages = rest[:n_pages]
    v_pages = rest[n_pages:2 * n_pages]
    o_ref = rest[2 * n_pages]
    del pt_ref
    past = n_pages * page
    n_rows = q_ref.shape[0]
    rows_per_head = GROUP * 2 * n_new
    r_idx = lax.broadcasted_iota(jnp.int32, (n_rows, 1), 0)
    h_row = r_idx // rows_per_head
    hg_row = r_idx // (2 * n_new)
    t_row = r_idx % n_new
    slope = jnp.zeros((n_rows, 1), _F32)
    for hg in range(N_HEADS):
        slope = jnp.where(hg_row == hg, SLOPES[hg], slope)
    q_pos = (past + t_row).astype(_F32)

    width = page * N_KV_HEADS
    col = lax.broadcasted_iota(jnp.int32, (1, width), 1)
    valid = (col % N_KV_HEADS) == h_row
    base = jnp.where(valid, slope * ((col // N_KV_HEADS).astype(_F32) - q_pos), NEG_BIG)
    q = q_ref[...]
    s_pages = [_dot_nt(q, kp[...].astype(_BF16)) + (base + slope * float(j * page))
               for j, kp in enumerate(k_pages)]

    n_cols_new = kn_ref.shape[0]
    col_n = lax.broadcasted_iota(jnp.int32, (1, n_cols_new), 1)
    t_key = col_n // N_KV_HEADS
    valid_n = ((col_n % N_KV_HEADS) == h_row) & (t_key <= t_row)
    s_new = _dot_nt(q, kn_ref[...].astype(_BF16))
    s_new = jnp.where(valid_n, s_new - slope * (t_row - t_key).astype(_F32), NEG_BIG)

    m = jnp.max(s_new, axis=-1, keepdims=True)
    for s in s_pages:
        m = jnp.maximum(m, jnp.max(s, axis=-1, keepdims=True))
    p_new = jnp.exp(s_new - m)
    l = jnp.sum(p_new, axis=-1, keepdims=True)
    acc = _dot(p_new.astype(_BF16), vn_ref[...].astype(_BF16))
    for s, vp in zip(s_pages, v_pages):
        p = jnp.exp(s - m)
        l = l + jnp.sum(p, axis=-1, keepdims=True)
        acc = acc + _dot(p.astype(_BF16), vp[...].astype(_BF16))
    on = acc / l
    lam = _lam_value(lq1_ref[...], lk1_ref[...], lq2_ref[...], lk2_ref[...], lam_init)
    o = on - lam * pltpu.roll(on, n_rows - n_new, 0)
    o_ref[...] = (_rms(o, gs_ref[...]) * (1.0 - lam_init)).astype(o_ref.dtype)


def _attn_sample_call(q, k_new, v_new, cache_k, cache_v, page_table, w, layer):
    n, n_rows, _ = q.shape
    n_new = n_rows // (N_KV_HEADS * GROUP * 2)
    n_pages = page_table.shape[1]
    page = cache_k.shape[2] // N_KV_HEADS
    small = lambda a: pl.BlockSpec((None,) + tuple(a.shape[1:]), lambda bi, pt: (layer, 0, 0))
    per_seq = lambda a: pl.BlockSpec((None,) + tuple(a.shape[1:]), lambda bi, pt: (bi, 0, 0))
    page_spec = lambda j: pl.BlockSpec((None, None) + tuple(cache_k.shape[2:]),
                                       lambda bi, pt: (layer, pt[bi * n_pages + j], 0, 0))
    grid_spec = pltpu.PrefetchScalarGridSpec(
        num_scalar_prefetch=1, grid=(n,),
        in_specs=[per_seq(q), per_seq(k_new), per_seq(v_new),
                  small(w['lam_q1']), small(w['lam_k1']), small(w['lam_q2']), small(w['lam_k2']),
                  small(w['g_subln'])]
                 + [page_spec(j) for j in range(n_pages)] + [page_spec(j) for j in range(n_pages)],
        out_specs=pl.BlockSpec((None, n_rows, LANES), lambda bi, pt: (bi, 0, 0)),
    )
    kern = functools.partial(_attn_sample_kernel, lam_init=_lam_init(layer), n_pages=n_pages, page=page,
                             n_new=n_new)
    return pl.pallas_call(
        kern, grid_spec=grid_spec,
        out_shape=jax.ShapeDtypeStruct((n, n_rows, LANES), _BF16),
        compiler_params=pltpu.CompilerParams(dimension_semantics=("arbitrary",),
                                             vmem_limit_bytes=VMEM_LIMIT_BYTES),
        name="attn_sample",
    )(page_table.reshape(-1), q, k_new, v_new, w['lam_q1'], w['lam_k1'], w['lam_q2'], w['lam_k2'],
      w['g_subln'], *([cache_k] * n_pages), *([cache_v] * n_pages))


def _post_kernel(h_ref, m_ref, g_ref, o_ref, p_ref, wba_ref, wo_ref, gffn_ref, wg_ref, wu_ref, wd_ref,
                 gple_ref, wpg_ref, wp_ref, gfin_ref, out_ref, *, is_last):
    mix = m_ref[...] + g_ref[...] * _dot(o_ref[...], wba_ref[...])
    h = h_ref[...] + _dot(mix.astype(_BF16), wo_ref[...])
    n2 = _rms(h, gffn_ref[...]).astype(_BF16)
    act = jax.nn.silu(_dot(n2, wg_ref[...])) * _dot(n2, wu_ref[...])
    h = h + _dot(act.astype(_BF16), wd_ref[...])
    n3 = _rms(h, gple_ref[...]).astype(_BF16)
    pg = jax.nn.sigmoid(_dot(n3, wpg_ref[...]))
    h = h + _dot(p_ref[...].astype(_BF16), wp_ref[...]) * pg
    out_ref[...] = _rms(h, gfin_ref[...]) if is_last else h


def _post_call(h, m_lru, g_att, o, p, w, layer, *, is_last):
    r, d = h.shape
    rows = min(ROW_BLOCK, r)
    assert r % rows == 0
    row_spec = lambda c: pl.BlockSpec((rows, c), lambda i: (i, 0))
    names = ['w_br_att', 'w_o', 'g_ffn', 'w_gate', 'w_up', 'w_down', 'g_ple', 'w_ple_gate', 'w_ple']
    kern = functools.partial(_post_kernel, is_last=is_last)
    return pl.pallas_call(
        kern, grid=(r // rows,),
        in_specs=[row_spec(d), row_spec(d), row_spec(d), row_spec(o.shape[-1]),
                  pl.BlockSpec((None, rows, p.shape[-1]), lambda i: (layer, i, 0))]
                 + [_const_spec(w[nm].shape, layer) for nm in names]
                 + [pl.BlockSpec(w['g_final'].shape, lambda i: (0, 0), pipeline_mode=pl.Buffered(1))],
        out_specs=row_spec(d),
        out_shape=jax.ShapeDtypeStruct((r, d), _F32),
        compiler_params=pltpu.CompilerParams(dimension_semantics=("arbitrary",),
                                             vmem_limit_bytes=VMEM_LIMIT_BYTES),
        name="post",
    )(h, m_lru, g_att, o, p, *[w[nm] for nm in names], w['g_final'])


def kernel(x_prompt, x_sample, p_prompt, p_sample, cache_k, cache_v, page_table, state_conv, state_h,
           w_in, g_mix, conv_w, conv_b, w_a, b_a, w_i, b_i, lru_lambda, lam_q1, lam_k1, lam_q2, lam_k2,
           g_subln, w_br_lru, w_br_att, w_o, g_ffn, w_gate, w_up, w_down, g_ple, w_ple_gate, w_ple, g_final):
    depth, d = g_mix.shape
    bp, t, _ = x_prompt.shape
    ns, ts, _ = x_sample.shape
    hist = CONV_W - 1
    row3 = lambda a: a.reshape(depth, 1, a.shape[-1])
    w = dict(
        w_in=w_in.astype(_BF16), g_mix=row3(g_mix), conv_w=conv_w, conv_b=row3(conv_b),
        w_ai=jnp.concatenate([w_a, w_i], axis=-1).astype(_BF16), b_a=row3(b_a), b_i=row3(b_i),
        lru_lambda=row3(lru_lambda), w_br_lru=w_br_lru.astype(_BF16),
        lam_q1=row3(lam_q1), lam_k1=row3(lam_k1), lam_q2=row3(lam_q2), lam_k2=row3(lam_k2),
        g_subln=row3(g_subln), w_br_att=w_br_att.astype(_BF16), w_o=w_o.astype(_BF16), g_ffn=row3(g_ffn),
        w_gate=w_gate.astype(_BF16), w_up=w_up.astype(_BF16), w_down=w_down.astype(_BF16),
        g_ple=row3(g_ple), w_ple_gate=w_ple_gate.astype(_BF16), w_ple=w_ple.astype(_BF16),
        g_final=g_final.reshape(1, d),
    )
    ck = cache_k.reshape(cache_k.shape[0], cache_k.shape[1], -1, cache_k.shape[-1])
    cv = cache_v.reshape(cache_v.shape[0], cache_v.shape[1], -1, cache_v.shape[-1])
    dk = N_KV_HEADS * 2 * D_HEAD
    dv = N_KV_HEADS * D_VHEAD

    tm = lambda a: jnp.swapaxes(a, -3, -2)
    hp = x_prompt
    hs = tm(x_sample).reshape(1, ts * ns, d)
    cs_all = tm(state_conv).reshape(depth, 1, hist * ns, d)
    ps_all = tm(p_sample).reshape(depth, ts * ns, -1)
    zero_c = jnp.zeros((bp, hist, d), _F32)
    zero_h = jnp.zeros((bp, 1, d), _F32)
    rows_p = min(ROW_BLOCK, t)
    pp_all = p_prompt.reshape(depth, bp * t, -1)

    cs_p, hs_p, ks_s, vs_s, cs_s, hs_s = ([] for _ in range(6))
    for l in range(depth):
        last = l == depth - 1
        q, k_stack, v_stack, m_lru, g_att, c_out, h_out, k_rdy, v_aug = _pre_call(
            hp, zero_c, zero_h, w, l, stride=1, rows=rows_p, is_prompt=True,
            kv_stack=None if l == 0 else (k_stack, v_stack))
        o = _attn_prompt_call(q, k_rdy, v_aug, w, l)
        hp = _post_call(hp.reshape(bp * t, d), m_lru.reshape(bp * t, d), g_att.reshape(bp * t, d),
                        o.reshape(bp * t, -1), pp_all, w, l,
                        is_last=last).reshape(bp, t, d)
        cs_p.append(c_out)
        hs_p.append(h_out.reshape(bp, d))
        q, k, v, m_lru, g_att, c_out, h_out = _pre_call(
            hs, cs_all[l], state_h[l][None], w, l, stride=ns, rows=ts * ns, is_prompt=False)
        qs = q.reshape(ts, ns, N_KV_HEADS, GROUP, 2, LANES).transpose(1, 2, 3, 4, 0, 5)
        qs = qs.reshape(ns, N_STREAMS * ts, LANES)
        k_sm = tm(k.reshape(ts, ns, dk))
        v_sm = tm(v.reshape(ts, ns, dv))
        o = _attn_sample_call(qs, k_sm.reshape(ns, ts * N_KV_HEADS, -1), v_sm.reshape(ns, ts * N_KV_HEADS, -1),
                              ck, cv, page_table, w, l)
        o = o.reshape(ns, N_KV_HEADS, GROUP, 2, ts, D_VHEAD)[:, :, :, 0]
        o = o.transpose(3, 0, 1, 2, 4).reshape(ts * ns, -1)
        hs = _post_call(hs.reshape(ts * ns, d), m_lru.reshape(ts * ns, d), g_att.reshape(ts * ns, d),
                        o, ps_all, w, l, is_last=last).reshape(1, ts * ns, d)
        ks_s.append(k_sm.reshape(ns, ts, N_KV_HEADS, 2 * D_HEAD))
        vs_s.append(v_sm.reshape(ns, ts, N_KV_HEADS, D_VHEAD))
        cs_s.append(tm(c_out.reshape(hist, ns, d)))
        hs_s.append(h_out.reshape(ns, d))

    y_sample = tm(hs.reshape(ts, ns, d))
    k_prompt = k_stack.reshape(depth, bp, t, N_KV_HEADS, 2 * D_HEAD)
    v_prompt = v_stack.reshape(depth, bp, t, N_KV_HEADS, D_VHEAD)
    return (hp, y_sample, k_prompt, v_prompt, jnp.stack(cs_p), jnp.stack(hs_p),
            jnp.stack(ks_s), jnp.stack(vs_s), jnp.stack(cs_s), jnp.stack(hs_s))
```

```python
import functools
import math

import jax
import jax.numpy as jnp
from jax import lax
from jax.experimental import pallas as pl
from jax.experimental.pallas import tpu as pltpu

_F32 = jnp.float32
_BF16 = jnp.bfloat16

N_HEADS = 8
N_KV_HEADS = 4
GROUP = N_HEADS // N_KV_HEADS
D_HEAD = 64
D_VHEAD = 2 * D_HEAD
N_STREAMS = N_HEADS * 2
N_LRU_BLOCKS = 8
CONV_W = 4
LRU_C = 8.0
EPS = 1e-6
SLOPES = tuple(2.0 ** (-8.0 * (i + 1) / N_HEADS) for i in range(N_HEADS))
NEG_BIG = -1e30

LANES = 128
SUBLANES = 8
VMEM_LIMIT_BYTES = 56 * 1024 * 1024

ROW_BLOCK = 512
ATTN_Q_BLOCK = 256
ATTN_K_BLOCK = 2048
POS_SPLIT = 64


def _lam_init(layer):
    return 0.8 - 0.6 * math.exp(-0.3 * layer)


def _rms(x, g):
    return x * lax.rsqrt(jnp.mean(x * x, axis=-1, keepdims=True) + EPS) * g


def _dot(a, b):
    return jnp.dot(a, b, preferred_element_type=_F32)


def _dot_nt(a, b):
    return lax.dot_general(a, b, (((1,), (1,)), ((), ())), preferred_element_type=_F32)


def _softplus(x):
    return jnp.maximum(x, 0.0) + jnp.log1p(jnp.exp(-jnp.abs(x)))


def _lam_value(q1, k1, q2, k2, lam_init):
    s1 = jnp.sum(q1 * k1, axis=-1, keepdims=True)
    s2 = jnp.sum(q2 * k2, axis=-1, keepdims=True)
    return jnp.exp(s1) - jnp.exp(s2) + lam_init


def _pre_kernel(x_ref, cs_ref, hs_ref, gmix_ref, w_in_ref, cw_ref, cb_ref, wai_ref, ba_ref, bi_ref,
                lam_ref, wbr_ref, *rest, stride, rows, pad, is_prompt, n_alias):
    rest = rest[n_alias:]
    if is_prompt:
        (q_ref, k_ref, v_ref, m_ref, g_ref, co_ref, ho_ref, kr_ref, va_ref,
         up_scr, a_scr, x_scr, gelu_scr, h_scr) = rest
    else:
        (q_ref, k_ref, v_ref, m_ref, g_ref, co_ref, ho_ref,
         up_scr, a_scr, x_scr, gelu_scr, h_scr) = rest
    d = x_ref.shape[-1]
    i = pl.program_id(1)
    hist = (CONV_W - 1) * stride

    n = _rms(x_ref[...], gmix_ref[...]).astype(_BF16)

    @pl.when(i == 0)
    def _():
        up_scr[pl.ds(pad - hist, hist), :] = cs_ref[...]
        h_scr[...] = hs_ref[...]

    o_q = 2 * d
    o_k = o_q + N_HEADS * 2 * D_HEAD
    o_v = o_k + N_KV_HEADS * 2 * D_HEAD
    o_g = o_v + N_KV_HEADS * D_VHEAD
    cw = 2 * LANES
    lane = lax.broadcasted_iota(jnp.int32, (1, LANES), 1)
    low = lane < D_HEAD
    proj = lambda off: _dot(n, w_in_ref[:, off:off + cw])
    if is_prompt:
        pos = i * rows + lax.broadcasted_iota(jnp.int32, (rows, 1), 0)
        hi = (pos // POS_SPLIT).astype(_F32)
        lo = (pos % POS_SPLIT).astype(_F32)
        k_extra = jnp.where(lane == D_HEAD, hi, jnp.where(lane == D_HEAD + 1, lo, 0.0))

    def gelu_task(c):
        gelu_scr[:, c * cw:(c + 1) * cw] = jax.nn.gelu(proj(d + c * cw))

    def q_task(c):
        qz = proj(o_q + c * cw) * (D_HEAD ** -0.5)
        for hg in range(2 * c, 2 * c + 2):
            chunk = qz[:, (hg - 2 * c) * LANES:(hg - 2 * c + 1) * LANES]
            if is_prompt:
                h, g = divmod(hg, GROUP)
                q_extra = jnp.where(lane == D_HEAD, SLOPES[hg] * POS_SPLIT,
                                    jnp.where(lane == D_HEAD + 1, SLOPES[hg], 0.0))
                q_ref[(2 * h) * GROUP + g] = jnp.where(low, chunk, q_extra).astype(_BF16)
                q_ref[(2 * h + 1) * GROUP + g] = jnp.where(
                    low, pltpu.roll(chunk, D_HEAD, 1), q_extra).astype(_BF16)
            else:
                q_ref[:, (2 * hg) * LANES:(2 * hg + 1) * LANES] = jnp.where(low, chunk, 0.0).astype(_BF16)
                q_ref[:, (2 * hg + 1) * LANES:(2 * hg + 2) * LANES] = jnp.where(low, 0.0, chunk).astype(_BF16)

    def k_task(c):
        kz = proj(o_k + c * cw)
        for h in range(2 * c, 2 * c + 2):
            chunk = kz[:, (h - 2 * c) * LANES:(h - 2 * c + 1) * LANES]
            if is_prompt:
                k_ref[pl.ds(h, rows, stride=N_KV_HEADS), :] = chunk
                kr_ref[2 * h] = jnp.where(low, chunk, k_extra).astype(_BF16)
                kr_ref[2 * h + 1] = jnp.where(low, pltpu.roll(chunk, D_HEAD, 1), k_extra).astype(_BF16)
            else:
                k_ref[:, h * LANES:(h + 1) * LANES] = chunk

    def v_task(c):
        vz = proj(o_v + c * cw)
        for h in range(2 * c, 2 * c + 2):
            chunk = vz[:, (h - 2 * c) * LANES:(h - 2 * c + 1) * LANES]
            if is_prompt:
                v_ref[pl.ds(h, rows, stride=N_KV_HEADS), :] = chunk
                va_ref[h, :, 0:D_VHEAD] = chunk.astype(_BF16)
                va_ref[h, :, D_VHEAD:2 * D_VHEAD] = jnp.ones((rows, LANES), _BF16)
            else:
                v_ref[:, h * LANES:(h + 1) * LANES] = chunk

    def gate_task(ref, off, c):
        ref[:, c * cw:(c + 1) * cw] = jax.nn.sigmoid(proj(off + c * cw))

    n_chunks = d // cw
    tasks = ([functools.partial(gelu_task, c) for c in range(n_chunks)]
             + [functools.partial(q_task, c) for c in range(N_HEADS // 2)]
             + [functools.partial(k_task, c) for c in range(N_KV_HEADS // 2)]
             + [functools.partial(v_task, c) for c in range(N_KV_HEADS // 2)]
             + [functools.partial(gate_task, m_ref, o_g, c) for c in range(n_chunks)]
             + [functools.partial(gate_task, g_ref, o_g + d, c) for c in range(n_chunks)])
    n_slots = N_LRU_BLOCKS + 1
    bounds = [len(tasks) * s // n_slots for s in range(n_slots + 1)]

    def run_tasks(slot):
        for task in tasks[bounds[slot]:bounds[slot + 1]]:
            task()

    up_scr[pl.ds(pad, rows), :] = _dot(n, w_in_ref[:, 0:d])
    run_tasks(0)
    uc = up_scr[pl.ds(pad - hist, rows), :] * cw_ref[0:1, :]
    for j in range(1, CONV_W):
        uc = uc + up_scr[pl.ds(pad - hist + j * stride, rows), :] * cw_ref[j:j + 1, :]
    uc = uc + cb_ref[...]
    tail = up_scr[pl.ds(pad + rows - hist, hist), :]
    co_ref[...] = tail
    up_scr[pl.ds(pad - hist, hist), :] = tail

    ucb = uc.astype(_BF16)
    blk = d // N_LRU_BLOCKS
    for nb in range(N_LRU_BLOCKS):
        sl = slice(nb * blk, (nb + 1) * blk)
        gates = _dot(ucb[:, sl], wai_ref[nb])
        r = jax.nn.sigmoid(gates[:, :blk] + ba_ref[:, sl])
        ig = jax.nn.sigmoid(gates[:, blk:] + bi_ref[:, sl])
        a = jnp.exp(-LRU_C * r * _softplus(-lam_ref[:, sl]))
        a_scr[:, sl] = a
        x_scr[:, sl] = jnp.sqrt(1.0 - a * a) * (ig * uc[:, sl])
        run_tasks(nb + 1)

    n_steps = rows // stride
    if stride == 1 and n_steps % 16 == 0:
        half = n_steps // 2

        def step2(t, carry):
            h1, x2, a2 = carry
            r1 = pl.ds(t, 1)
            r2 = pl.ds(half + t, 1)
            h1 = a_scr[r1, :] * h1 + x_scr[r1, :]
            a_row = a_scr[r2, :]
            x2 = a_row * x2 + x_scr[r2, :]
            a2 = a_row * a2
            x_scr[r1, :] = h1
            x_scr[r2, :] = x2
            a_scr[r2, :] = a2
            return h1, x2, a2

        h_mid, x2, a2 = lax.fori_loop(0, half, step2,
                                      (h_scr[...], jnp.zeros((1, d), _F32), jnp.ones((1, d), _F32)), unroll=8)
        x_scr[half:, :] = x_scr[half:, :] + a_scr[half:, :] * h_mid
        h_last = x2 + a2 * h_mid
    else:
        def step(t, h):
            sl = pl.ds(pl.multiple_of(t * stride, stride), stride)
            h = a_scr[sl, :] * h + x_scr[sl, :]
            x_scr[sl, :] = h
            return h

        h_last = lax.fori_loop(0, n_steps, step, h_scr[...], unroll=min(8, n_steps))
    h_scr[...] = h_last
    ho_ref[...] = h_last

    b_lru = _dot((x_scr[...] * gelu_scr[...]).astype(_BF16), wbr_ref[...])
    m_ref[...] = m_ref[...] * b_lru


def _const_spec(shape, layer):
    nd = len(shape)
    return pl.BlockSpec((None,) + tuple(shape[1:]), lambda *_: (layer,) + (0,) * (nd - 1),
                        pipeline_mode=pl.Buffered(1))


def _pre_call(x, conv_state, h_state, w, layer, *, stride, rows, is_prompt, kv_stack=None):
    b, r, d = x.shape
    assert r % rows == 0 and rows % stride == 0 and rows >= (CONV_W - 1) * stride
    hist = (CONV_W - 1) * stride
    pad = -(-hist // SUBLANES) * SUBLANES
    grid = (b, r // rows)
    depth = w['g_mix'].shape[0]
    row_spec = lambda c: pl.BlockSpec((None, rows, c), lambda bi, i: (bi, i, 0))
    state_spec = lambda c: pl.BlockSpec((None, c, d), lambda bi, i: (bi, 0, 0))
    in_specs = [
        row_spec(d), state_spec(hist), state_spec(stride),
        _const_spec(w['g_mix'].shape, layer), _const_spec(w['w_in'].shape, layer),
        _const_spec(w['conv_w'].shape, layer), _const_spec(w['conv_b'].shape, layer),
        _const_spec(w['w_ai'].shape, layer), _const_spec(w['b_a'].shape, layer),
        _const_spec(w['b_i'].shape, layer), _const_spec(w['lru_lambda'].shape, layer),
        _const_spec(w['w_br_lru'].shape, layer),
    ]
    args = [x, conv_state, h_state, w['g_mix'], w['w_in'], w['conv_w'], w['conv_b'], w['w_ai'], w['b_a'],
            w['b_i'], w['lru_lambda'], w['w_br_lru']]
    aliases = {}
    if kv_stack is not None:
        aliases = {len(args): 1, len(args) + 1: 2}
        args += list(kv_stack)
        in_specs += [pl.BlockSpec(memory_space=pl.ANY)] * 2
    dq = N_STREAMS * LANES
    dk = N_KV_HEADS * 2 * D_HEAD
    dv = N_KV_HEADS * D_VHEAD
    head_spec = lambda nh, c: pl.BlockSpec((None, nh, rows, c), lambda bi, i: (bi, 0, i, 0))
    if is_prompt:
        kv_shape = jax.ShapeDtypeStruct((depth, b, r * N_KV_HEADS, LANES), _F32)
        kv_spec = pl.BlockSpec((None, None, rows * N_KV_HEADS, LANES), lambda bi, i: (layer, bi, i, 0))
        out_shape = [jax.ShapeDtypeStruct((b, N_STREAMS, r, LANES), _BF16), kv_shape, kv_shape]
        out_specs = [head_spec(N_STREAMS, LANES), kv_spec, kv_spec]
    else:
        out_shape = [jax.ShapeDtypeStruct((b, r, dq), _BF16), jax.ShapeDtypeStruct((b, r, dk), _F32),
                     jax.ShapeDtypeStruct((b, r, dv), _F32)]
        out_specs = [row_spec(dq), row_spec(dk), row_spec(dv)]
    out_shape += [jax.ShapeDtypeStruct((b, r, d), _F32), jax.ShapeDtypeStruct((b, r, d), _F32),
                  jax.ShapeDtypeStruct((b, hist, d), _F32), jax.ShapeDtypeStruct((b, stride, d), _F32)]
    out_specs += [row_spec(d), row_spec(d), state_spec(hist), state_spec(stride)]
    if is_prompt:
        out_shape += [jax.ShapeDtypeStruct((b, 2 * N_KV_HEADS, r, LANES), _BF16),
                      jax.ShapeDtypeStruct((b, N_KV_HEADS, r, 2 * D_VHEAD), _BF16)]
        out_specs += [head_spec(2 * N_KV_HEADS, LANES), head_spec(N_KV_HEADS, 2 * D_VHEAD)]
    kern = functools.partial(_pre_kernel, stride=stride, rows=rows, pad=pad, is_prompt=is_prompt,
                             n_alias=len(aliases))
    return pl.pallas_call(
        kern, grid=grid, in_specs=in_specs, out_specs=out_specs, out_shape=out_shape,
        input_output_aliases=aliases,
        scratch_shapes=[pltpu.VMEM((pad + rows, d), _F32), pltpu.VMEM((rows, d), _F32),
                        pltpu.VMEM((rows, d), _F32), pltpu.VMEM((rows, d), _F32),
                        pltpu.VMEM((stride, d), _F32)],
        compiler_params=pltpu.CompilerParams(dimension_semantics=("arbitrary", "arbitrary"),
                                             vmem_limit_bytes=VMEM_LIMIT_BYTES),
        name="pre_prompt" if is_prompt else "pre_sample",
    )(*args)


def _attn_prompt_kernel(qt_ref, kt_ref, q_ref, k_ref, v_ref, lq1_ref, lk1_ref, lq2_ref, lk2_ref, gs_ref,
                        o_ref, m_scr, acc_scr, *, lam_init, tq, tk):
    s_idx = pl.program_id(1)
    qi = qt_ref[s_idx]
    kj = kt_ref[s_idx]
    rows = GROUP * tq
    n_pairs = 2 * N_KV_HEADS

    @pl.when(kj == 0)
    def _():
        m_scr[...] = jnp.full(m_scr.shape, NEG_BIG, _F32)
        acc_scr[...] = jnp.zeros(acc_scr.shape, _F32)

    def update(n_cols, masked):
        if masked:
            row = lax.broadcasted_iota(jnp.int32, (rows, tq), 0) % tq
            col = lax.broadcasted_iota(jnp.int32, (rows, tq), 1)
            keep = col <= row
        def scores(pair):
            q = q_ref[GROUP * pair:GROUP * (pair + 1)].reshape(rows, LANES)
            return _dot_nt(q, k_ref[pair, 0:n_cols, :])

        s_next = scores(0)
        for pair in range(n_pairs):
            s = s_next
            if pair + 1 < n_pairs:
                s_next = scores(pair + 1)
            if masked:
                diag = jnp.where(keep, s[:, n_cols - tq:], NEG_BIG)
                s = diag if n_cols == tq else jnp.concatenate([s[:, :n_cols - tq], diag], axis=1)
            m_prev = m_scr[pair]
            m_new = jnp.maximum(m_prev, jnp.max(s, axis=1, keepdims=True))
            alpha = jnp.exp(m_prev - m_new)
            p = jnp.exp(s - jnp.concatenate([m_new] * (n_cols // LANES), axis=1))
            pv = _dot(p.astype(_BF16), v_ref[pair // 2, 0:n_cols, :])
            acc_scr[pair] = jnp.concatenate([alpha, alpha], axis=1) * acc_scr[pair] + pv
            m_scr[pair] = m_new

    n_sub = tk // tq
    is_last = kj == qi // n_sub

    @pl.when(jnp.logical_not(is_last))
    def _():
        update(tk, False)

    for r in range(n_sub):
        @pl.when(jnp.logical_and(is_last, qi % n_sub == r))
        def _():
            update((r + 1) * tq, True)

    @pl.when(is_last)
    def _():
        lam = _lam_value(lq1_ref[...], lk1_ref[...], lq2_ref[...], lk2_ref[...], lam_init)
        for hg in range(N_HEADS):
            h, g = divmod(hg, GROUP)
            a0 = acc_scr[2 * h, g * tq:(g + 1) * tq, :]
            a1 = acc_scr[2 * h + 1, g * tq:(g + 1) * tq, :]
            o = a0[:, :D_VHEAD] / a0[:, D_VHEAD:] - lam * (a1[:, :D_VHEAD] / a1[:, D_VHEAD:])
            o_ref[:, hg * LANES:(hg + 1) * LANES] = (_rms(o, gs_ref[...]) * (1.0 - lam_init)).astype(o_ref.dtype)


def _attn_prompt_call(q, k, v, w, layer):
    b, _, t, _ = q.shape
    tq = min(ATTN_Q_BLOCK, t)
    tk = min(ATTN_K_BLOCK, t)
    assert t % tk == 0 and tk % tq == 0 and t <= POS_SPLIT * 256
    pairs = [(qi, kj) for qi in range(t // tq) for kj in range((qi * tq) // tk + 1)]
    qt = jnp.asarray([p[0] for p in pairs], jnp.int32)
    kt = jnp.asarray([p[1] for p in pairs], jnp.int32)
    small = lambda a: pl.BlockSpec((None,) + tuple(a.shape[1:]), lambda bi, s, qt, kt: (layer, 0, 0))
    grid_spec = pltpu.PrefetchScalarGridSpec(
        num_scalar_prefetch=2, grid=(b, len(pairs)),
        in_specs=[
            pl.BlockSpec((None, q.shape[1], tq, LANES), lambda bi, s, qt, kt: (bi, 0, qt[s], 0)),
            pl.BlockSpec((None, k.shape[1], tk, LANES), lambda bi, s, qt, kt: (bi, 0, kt[s], 0)),
            pl.BlockSpec((None, v.shape[1], tk, v.shape[-1]), lambda bi, s, qt, kt: (bi, 0, kt[s], 0)),
            small(w['lam_q1']), small(w['lam_k1']), small(w['lam_q2']), small(w['lam_k2']),
            small(w['g_subln']),
        ],
        out_specs=pl.BlockSpec((None, tq, N_HEADS * D_VHEAD), lambda bi, s, qt, kt: (bi, qt[s], 0)),
        scratch_shapes=[pltpu.VMEM((2 * N_KV_HEADS, GROUP * tq, LANES), _F32),
                        pltpu.VMEM((2 * N_KV_HEADS, GROUP * tq, 2 * D_VHEAD), _F32)],
    )
    kern = functools.partial(_attn_prompt_kernel, lam_init=_lam_init(layer), tq=tq, tk=tk)
    return pl.pallas_call(
        kern, grid_spec=grid_spec,
        out_shape=jax.ShapeDtypeStruct((b, t, N_HEADS * D_VHEAD), _BF16),
        compiler_params=pltpu.CompilerParams(dimension_semantics=("arbitrary", "arbitrary"),
                                             vmem_limit_bytes=VMEM_LIMIT_BYTES),
        name="attn_prompt",
    )(qt, kt, q, k, v, w['lam_q1'], w['lam_k1'], w['lam_q2'], w['lam_k2'], w['g_subln'])


def _attn_sample_kernel(pt_ref, q_ref, kn_ref, vn_ref, lq1_ref, lk1_ref, lq2_ref, lk2_ref, gs_ref, *rest,
                        lam_init, n_pages, page, n_new):
    k_pages = rest[:n_pages]
    v_pages = rest[n_pages:2 * n_pages]
    o_ref = rest[2 * n_pages]
    del pt_ref
    past = n_pages * page
    n_rows = q_ref.shape[0]
    rows_per_head = GROUP * 2 * n_new
    r_idx = lax.broadcasted_iota(jnp.int32, (n_rows, 1), 0)
    h_row = r_idx // rows_per_head
    hg_row = r_idx // (2 * n_new)
    t_row = r_idx % n_new
    slope = jnp.zeros((n_rows, 1), _F32)
    for hg in range(N_HEADS):
        slope = jnp.where(hg_row == hg, SLOPES[hg], slope)
    q_pos = (past + t_row).astype(_F32)

    width = page * N_KV_HEADS
    col = lax.broadcasted_iota(jnp.int32, (1, width), 1)
    valid = (col % N_KV_HEADS) == h_row
    base = jnp.where(valid, slope * ((col // N_KV_HEADS).astype(_F32) - q_pos), NEG_BIG)
    q = q_ref[...]
    s_pages = [_dot_nt(q, kp[...].astype(_BF16)) + (base + slope * float(j * page))
               for j, kp in enumerate(k_pages)]

    n_cols_new = kn_ref.shape[0]
    col_n = lax.broadcasted_iota(jnp.int32, (1, n_cols_new), 1)
    t_key = col_n // N_KV_HEADS
    valid_n = ((col_n % N_KV_HEADS) == h_row) & (t_key <= t_row)
    s_new = _dot_nt(q, kn_ref[...].astype(_BF16))
    s_new = jnp.where(valid_n, s_new - slope * (t_row - t_key).astype(_F32), NEG_BIG)

    m = jnp.max(s_new, axis=-1, keepdims=True)
    for s in s_pages:
        m = jnp.maximum(m, jnp.max(s, axis=-1, keepdims=True))
    p_new = jnp.exp(s_new - m)
    l = jnp.sum(p_new, axis=-1, keepdims=True)
    acc = _dot(p_new.astype(_BF16), vn_ref[...].astype(_BF16))
    for s, vp in zip(s_pages, v_pages):
        p = jnp.exp(s - m)
        l = l + jnp.sum(p, axis=-1, keepdims=True)
        acc = acc + _dot(p.astype(_BF16), vp[...].astype(_BF16))
    on = acc / l
    lam = _lam_value(lq1_ref[...], lk1_ref[...], lq2_ref[...], lk2_ref[...], lam_init)
    o = on - lam * pltpu.roll(on, n_rows - n_new, 0)
    o_ref[...] = (_rms(o, gs_ref[...]) * (1.0 - lam_init)).astype(o_ref.dtype)


def _attn_sample_call(q, k_new, v_new, cache_k, cache_v, page_table, w, layer):
    n, n_rows, _ = q.shape
    n_new = n_rows // (N_KV_HEADS * GROUP * 2)
    n_pages = page_table.shape[1]
    page = cache_k.shape[2] // N_KV_HEADS
    small = lambda a: pl.BlockSpec((None,) + tuple(a.shape[1:]), lambda bi, pt: (layer, 0, 0))
    per_seq = lambda a: pl.BlockSpec((None,) + tuple(a.shape[1:]), lambda bi, pt: (bi, 0, 0))
    page_spec = lambda j: pl.BlockSpec((None, None) + tuple(cache_k.shape[2:]),
                                       lambda bi, pt: (layer, pt[bi * n_pages + j], 0, 0))
    grid_spec = pltpu.PrefetchScalarGridSpec(
        num_scalar_prefetch=1, grid=(n,),
        in_specs=[per_seq(q), per_seq(k_new), per_seq(v_new),
                  small(w['lam_q1']), small(w['lam_k1']), small(w['lam_q2']), small(w['lam_k2']),
                  small(w['g_subln'])]
                 + [page_spec(j) for j in range(n_pages)] + [page_spec(j) for j in range(n_pages)],
        out_specs=pl.BlockSpec((None, n_rows, LANES), lambda bi, pt: (bi, 0, 0)),
    )
    kern = functools.partial(_attn_sample_kernel, lam_init=_lam_init(layer), n_pages=n_pages, page=page,
                             n_new=n_new)
    return pl.pallas_call(
        kern, grid_spec=grid_spec,
        out_shape=jax.ShapeDtypeStruct((n, n_rows, LANES), _BF16),
        compiler_params=pltpu.CompilerParams(dimension_semantics=("arbitrary",),
                                             vmem_limit_bytes=VMEM_LIMIT_BYTES),
        name="attn_sample",
    )(page_table.reshape(-1), q, k_new, v_new, w['lam_q1'], w['lam_k1'], w['lam_q2'], w['lam_k2'],
      w['g_subln'], *([cache_k] * n_pages), *([cache_v] * n_pages))


def _post_kernel(h_ref, m_ref, g_ref, o_ref, p_ref, wba_ref, wo_ref, gffn_ref, wg_ref, wu_ref, wd_ref,
                 gple_ref, wpg_ref, wp_ref, gfin_ref, out_ref, *, is_last):
    mix = m_ref[...] + g_ref[...] * _dot(o_ref[...], wba_ref[...])
    h = h_ref[...] + _dot(mix.astype(_BF16), wo_ref[...])
    n2 = _rms(h, gffn_ref[...]).astype(_BF16)
    act = jax.nn.silu(_dot(n2, wg_ref[...])) * _dot(n2, wu_ref[...])
    h = h + _dot(act.astype(_BF16), wd_ref[...])
    n3 = _rms(h, gple_ref[...]).astype(_BF16)
    pg = jax.nn.sigmoid(_dot(n3, wpg_ref[...]))
    h = h + _dot(p_ref[...].astype(_BF16), wp_ref[...]) * pg
    out_ref[...] = _rms(h, gfin_ref[...]) if is_last else h


def _post_call(h, m_lru, g_att, o, p, w, layer, *, is_last):
    r, d = h.shape
    rows = min(ROW_BLOCK, r)
    assert r % rows == 0
    row_spec = lambda c: pl.BlockSpec((rows, c), lambda i: (i, 0))
    names = ['w_br_att', 'w_o', 'g_ffn', 'w_gate', 'w_up', 'w_down', 'g_ple', 'w_ple_gate', 'w_ple']
    kern = functools.partial(_post_kernel, is_last=is_last)
    return pl.pallas_call(
        kern, grid=(r // rows,),
        in_specs=[row_spec(d), row_spec(d), row_spec(d), row_spec(o.shape[-1]),
                  pl.BlockSpec((None, rows, p.shape[-1]), lambda i: (layer, i, 0))]
                 + [_const_spec(w[nm].shape, layer) for nm in names]
                 + [pl.BlockSpec(w['g_final'].shape, lambda i: (0, 0), pipeline_mode=pl.Buffered(1))],
        out_specs=row_spec(d),
        out_shape=jax.ShapeDtypeStruct((r, d), _F32),
        compiler_params=pltpu.CompilerParams(dimension_semantics=("arbitrary",),
                                             vmem_limit_bytes=VMEM_LIMIT_BYTES),
        name="post",
    )(h, m_lru, g_att, o, p, *[w[nm] for nm in names], w['g_final'])


def kernel(x_prompt, x_sample, p_prompt, p_sample, cache_k, cache_v, page_table, state_conv, state_h,
           w_in, g_mix, conv_w, conv_b, w_a, b_a, w_i, b_i, lru_lambda, lam_q1, lam_k1, lam_q2, lam_k2,
           g_subln, w_br_lru, w_br_att, w_o, g_ffn, w_gate, w_up, w_down, g_ple, w_ple_gate, w_ple, g_final):
    depth, d = g_mix.shape
    bp, t, _ = x_prompt.shape
    ns, ts, _ = x_sample.shape
    hist = CONV_W - 1
    row3 = lambda a: a.reshape(depth, 1, a.shape[-1])
    w = dict(
        w_in=w_in.astype(_BF16), g_mix=row3(g_mix), conv_w=conv_w, conv_b=row3(conv_b),
        w_ai=jnp.concatenate([w_a, w_i], axis=-1).astype(_BF16), b_a=row3(b_a), b_i=row3(b_i),
        lru_lambda=row3(lru_lambda), w_br_lru=w_br_lru.astype(_BF16),
        lam_q1=row3(lam_q1), lam_k1=row3(lam_k1), lam_q2=row3(lam_q2), lam_k2=row3(lam_k2),
        g_subln=row3(g_subln), w_br_att=w_br_att.astype(_BF16), w_o=w_o.astype(_BF16), g_ffn=row3(g_ffn),
        w_gate=w_gate.astype(_BF16), w_up=w_up.astype(_BF16), w_down=w_down.astype(_BF16),
        g_ple=row3(g_ple), w_ple_gate=w_ple_gate.astype(_BF16), w_ple=w_ple.astype(_BF16),
        g_final=g_final.reshape(1, d),
    )
    ck = cache_k.reshape(cache_k.shape[0], cache_k.shape[1], -1, cache_k.shape[-1])
    cv = cache_v.reshape(cache_v.shape[0], cache_v.shape[1], -1, cache_v.shape[-1])
    dk = N_KV_HEADS * 2 * D_HEAD
    dv = N_KV_HEADS * D_VHEAD

    tm = lambda a: jnp.swapaxes(a, -3, -2)
    hp = x_prompt
    hs = tm(x_sample).reshape(1, ts * ns, d)
    cs_all = tm(state_conv).reshape(depth, 1, hist * ns, d)
    ps_all = tm(p_sample).reshape(depth, ts * ns, -1)
    zero_c = jnp.zeros((bp, hist, d), _F32)
    zero_h = jnp.zeros((bp, 1, d), _F32)
    rows_p = min(ROW_BLOCK, t)
    pp_all = p_prompt.reshape(depth, bp * t, -1)

    cs_p, hs_p, ks_s, vs_s, cs_s, hs_s = ([] for _ in range(6))
    for l in range(depth):
        last = l == depth - 1
        q, k_stack, v_stack, m_lru, g_att, c_out, h_out, k_rdy, v_aug = _pre_call(
            hp, zero_c, zero_h, w, l, stride=1, rows=rows_p, is_prompt=True,
            kv_stack=None if l == 0 else (k_stack, v_stack))
        o = _attn_prompt_call(q, k_rdy, v_aug, w, l)
        hp = _post_call(hp.reshape(bp * t, d), m_lru.reshape(bp * t, d), g_att.reshape(bp * t, d),
                        o.reshape(bp * t, -1), pp_all, w, l,
                        is_last=last).reshape(bp, t, d)
        cs_p.append(c_out)
        hs_p.append(h_out.reshape(bp, d))
        q, k, v, m_lru, g_att, c_out, h_out = _pre_call(
            hs, cs_all[l], state_h[l][None], w, l, stride=ns, rows=ts * ns, is_prompt=False)
        qs = q.reshape(ts, ns, N_KV_HEADS, GROUP, 2, LANES).transpose(1, 2, 3, 4, 0, 5)
        qs = qs.reshape(ns, N_STREAMS * ts, LANES)
        k_sm = tm(k.reshape(ts, ns, dk))
        v_sm = tm(v.reshape(ts, ns, dv))
        o = _attn_sample_call(qs, k_sm.reshape(ns, ts * N_KV_HEADS, -1), v_sm.reshape(ns, ts * N_KV_HEADS, -1),
                              ck, cv, page_table, w, l)
        o = o.reshape(ns, N_KV_HEADS, GROUP, 2, ts, D_VHEAD)[:, :, :, 0]
        o = o.transpose(3, 0, 1, 2, 4).reshape(ts * ns, -1)
        hs = _post_call(hs.reshape(ts * ns, d), m_lru.reshape(ts * ns, d), g_att.reshape(ts * ns, d),
                        o, ps_all, w, l, is_last=last).reshape(1, ts * ns, d)
        ks_s.append(k_sm.reshape(ns, ts, N_KV_HEADS, 2 * D_HEAD))
        vs_s.append(v_sm.reshape(ns, ts, N_KV_HEADS, D_VHEAD))
        cs_s.append(tm(c_out.reshape(hist, ns, d)))
        hs_s.append(h_out.reshape(ns, d))

    y_sample = tm(hs.reshape(ts, ns, d))
    k_prompt = k_stack.reshape(depth, bp, t, N_KV_HEADS, 2 * D_HEAD)
    v_prompt = v_stack.reshape(depth, bp, t, N_KV_HEADS, D_VHEAD)
    return (hp, y_sample, k_prompt, v_prompt, jnp.stack(cs_p), jnp.stack(hs_p),
            jnp.stack(ks_s), jnp.stack(vs_s), jnp.stack(cs_s), jnp.stack(hs_s))
```

```python
import functools
import math

import jax
import jax.numpy as jnp
import numpy as np
from jax import lax
from jax.experimental import pallas as pl
from jax.experimental.pallas import tpu as pltpu

_F32 = jnp.float32
_BF16 = jnp.bfloat16

N_HEADS = 8
N_KV_HEADS = 4
GROUP = N_HEADS // N_KV_HEADS
D_HEAD = 64
D_VHEAD = 2 * D_HEAD
N_STREAMS = N_HEADS * 2
N_LRU_BLOCKS = 8
CONV_W = 4
LRU_C = 8.0
EPS = 1e-6
SLOPES = tuple(2.0 ** (-8.0 * (i + 1) / N_HEADS) for i in range(N_HEADS))
NEG_BIG = -1e30

LANES = 128
SUBLANES = 8
VMEM_LIMIT_BYTES = 56 * 1024 * 1024

ROW_BLOCK = 512
ATTN_Q_BLOCK = 256
ATTN_K_BLOCK = 1024
POS_SPLIT = 64
LOG2E = math.log2(math.e)
N_PIECES = 3


def _bf16_pieces(value):
    pieces, rest = [], float(value)
    for _ in range(N_PIECES):
        piece = float(np.asarray(rest, np.float32).astype(_BF16).astype(np.float64))
        pieces.append(piece)
        rest -= piece
    return tuple(pieces)


SLOPE_PIECES = tuple(_bf16_pieces(LOG2E * s) for s in SLOPES)


def _lam_init(layer):
    return 0.8 - 0.6 * math.exp(-0.3 * layer)


def _rms(x, g):
    return x * lax.rsqrt(jnp.mean(x * x, axis=-1, keepdims=True) + EPS) * g


def _dot(a, b):
    return jnp.dot(a, b, preferred_element_type=_F32)


def _dot_nt(a, b):
    return lax.dot_general(a, b, (((1,), (1,)), ((), ())), preferred_element_type=_F32)


def _softplus(x):
    return jnp.maximum(x, 0.0) + jnp.log1p(jnp.exp(-jnp.abs(x)))


def _lam_value(q1, k1, q2, k2, lam_init):
    s1 = jnp.sum(q1 * k1, axis=-1, keepdims=True)
    s2 = jnp.sum(q2 * k2, axis=-1, keepdims=True)
    return jnp.exp(s1) - jnp.exp(s2) + lam_init


def _pre_kernel(x_ref, cs_ref, hs_ref, gmix_ref, w_in_ref, cw_ref, cb_ref, wai_ref, ba_ref, bi_ref,
                lam_ref, wbr_ref, *rest, stride, rows, pad, is_prompt, n_alias):
    rest = rest[n_alias:]
    if is_prompt:
        (q_ref, k_ref, v_ref, m_ref, g_ref, co_ref, ho_ref, kr_ref, va_ref,
         up_scr, a_scr, x_scr, gelu_scr, h_scr) = rest
    else:
        (q_ref, k_ref, v_ref, m_ref, g_ref, co_ref, ho_ref,
         up_scr, a_scr, x_scr, gelu_scr, h_scr) = rest
    d = x_ref.shape[-1]
    i = pl.program_id(1)
    hist = (CONV_W - 1) * stride

    n = _rms(x_ref[...], gmix_ref[...]).astype(_BF16)

    @pl.when(i == 0)
    def _():
        up_scr[pl.ds(pad - hist, hist), :] = cs_ref[...]
        h_scr[...] = hs_ref[...]

    o_q = 2 * d
    o_k = o_q + N_HEADS * 2 * D_HEAD
    o_v = o_k + N_KV_HEADS * 2 * D_HEAD
    o_g = o_v + N_KV_HEADS * D_VHEAD
    cw = 2 * LANES
    lane = lax.broadcasted_iota(jnp.int32, (1, LANES), 1)
    low = lane < D_HEAD
    proj = lambda off: _dot(n, w_in_ref[:, off:off + cw])
    if is_prompt:
        pos = i * rows + lax.broadcasted_iota(jnp.int32, (rows, 1), 0)
        hi = (pos // POS_SPLIT).astype(_F32)
        lo = (pos % POS_SPLIT).astype(_F32)
        in_hi = (lane >= D_HEAD) & (lane < D_HEAD + N_PIECES)
        in_lo = (lane >= D_HEAD + N_PIECES) & (lane < D_HEAD + 2 * N_PIECES)
        k_extra = jnp.where(in_hi, hi, jnp.where(in_lo, lo, 0.0))
        q_scale = LOG2E * D_HEAD ** -0.5
    else:
        q_scale = D_HEAD ** -0.5

    def q_extra_lanes(hg):
        extra = jnp.zeros((1, LANES), _F32)
        for j, piece in enumerate(SLOPE_PIECES[hg]):
            extra = jnp.where(lane == D_HEAD + j, piece * POS_SPLIT, extra)
            extra = jnp.where(lane == D_HEAD + N_PIECES + j, piece, extra)
        return extra

    def gelu_task(c):
        gelu_scr[:, c * cw:(c + 1) * cw] = jax.nn.gelu(proj(d + c * cw))

    def q_task(c):
        qz = proj(o_q + c * cw) * q_scale
        for hg in range(2 * c, 2 * c + 2):
            chunk = qz[:, (hg - 2 * c) * LANES:(hg - 2 * c + 1) * LANES]
            if is_prompt:
                h, g = divmod(hg, GROUP)
                q_extra = q_extra_lanes(hg)
                q_ref[(2 * h) * GROUP + g] = jnp.where(low, chunk, q_extra).astype(_BF16)
                q_ref[(2 * h + 1) * GROUP + g] = jnp.where(
                    low, pltpu.roll(chunk, D_HEAD, 1), q_extra).astype(_BF16)
            else:
                q_ref[:, (2 * hg) * LANES:(2 * hg + 1) * LANES] = jnp.where(low, chunk, 0.0).astype(_BF16)
                q_ref[:, (2 * hg + 1) * LANES:(2 * hg + 2) * LANES] = jnp.where(low, 0.0, chunk).astype(_BF16)

    def k_task(c):
        kz = proj(o_k + c * cw)
        for h in range(2 * c, 2 * c + 2):
            chunk = kz[:, (h - 2 * c) * LANES:(h - 2 * c + 1) * LANES]
            if is_prompt:
                k_ref[pl.ds(h, rows, stride=N_KV_HEADS), :] = chunk
                kr_ref[2 * h] = jnp.where(low, chunk, k_extra).astype(_BF16)
                kr_ref[2 * h + 1] = jnp.where(low, pltpu.roll(chunk, D_HEAD, 1), k_extra).astype(_BF16)
            else:
                k_ref[:, h * LANES:(h + 1) * LANES] = chunk

    def v_task(c):
        vz = proj(o_v + c * cw)
        for h in range(2 * c, 2 * c + 2):
            chunk = vz[:, (h - 2 * c) * LANES:(h - 2 * c + 1) * LANES]
            if is_prompt:
                v_ref[pl.ds(h, rows, stride=N_KV_HEADS), :] = chunk
                va_ref[h, :, 0:D_VHEAD] = chunk.astype(_BF16)
                va_ref[h, :, D_VHEAD:2 * D_VHEAD] = jnp.ones((rows, LANES), _BF16)
            else:
                v_ref[:, h * LANES:(h + 1) * LANES] = chunk

    def gate_task(ref, off, c):
        ref[:, c * cw:(c + 1) * cw] = jax.nn.sigmoid(proj(off + c * cw))

    n_chunks = d // cw
    tasks = ([functools.partial(gelu_task, c) for c in range(n_chunks)]
             + [functools.partial(q_task, c) for c in range(N_HEADS // 2)]
             + [functools.partial(k_task, c) for c in range(N_KV_HEADS // 2)]
             + [functools.partial(v_task, c) for c in range(N_KV_HEADS // 2)]
             + [functools.partial(gate_task, m_ref, o_g, c) for c in range(n_chunks)]
             + [functools.partial(gate_task, g_ref, o_g + d, c) for c in range(n_chunks)])
    n_slots = N_LRU_BLOCKS + 1
    bounds = [len(tasks) * s // n_slots for s in range(n_slots + 1)]

    def run_tasks(slot):
        for task in tasks[bounds[slot]:bounds[slot + 1]]:
            task()

    up_scr[pl.ds(pad, rows), :] = _dot(n, w_in_ref[:, 0:d])
    run_tasks(0)
    uc = up_scr[pl.ds(pad - hist, rows), :] * cw_ref[0:1, :]
    for j in range(1, CONV_W):
        uc = uc + up_scr[pl.ds(pad - hist + j * stride, rows), :] * cw_ref[j:j + 1, :]
    uc = uc + cb_ref[...]
    tail = up_scr[pl.ds(pad + rows - hist, hist), :]
    co_ref[...] = tail
    up_scr[pl.ds(pad - hist, hist), :] = tail

    ucb = uc.astype(_BF16)
    blk = d // N_LRU_BLOCKS
    for nb in range(N_LRU_BLOCKS):
        sl = slice(nb * blk, (nb + 1) * blk)
        gates = _dot(ucb[:, sl], wai_ref[nb])
        r = jax.nn.sigmoid(gates[:, :blk] + ba_ref[:, sl])
        ig = jax.nn.sigmoid(gates[:, blk:] + bi_ref[:, sl])
        a = jnp.exp(-LRU_C * r * _softplus(-lam_ref[:, sl]))
        a_scr[:, sl] = a
        x_scr[:, sl] = jnp.sqrt(1.0 - a * a) * (ig * uc[:, sl])
        run_tasks(nb + 1)

    n_steps = rows // stride
    if stride == 1 and n_steps % 16 == 0:
        half = n_steps // 2

        def step2(t, carry):
            h1, x2, a2 = carry
            r1 = pl.ds(t, 1)
            r2 = pl.ds(half + t, 1)
            h1 = a_scr[r1, :] * h1 + x_scr[r1, :]
            a_row = a_scr[r2, :]
            x2 = a_row * x2 + x_scr[r2, :]
            a2 = a_row * a2
            x_scr[r1, :] = h1
            x_scr[r2, :] = x2
            a_scr[r2, :] = a2
            return h1, x2, a2

        h_mid, x2, a2 = lax.fori_loop(0, half, step2,
                                      (h_scr[...], jnp.zeros((1, d), _F32), jnp.ones((1, d), _F32)), unroll=8)
        x_scr[half:, :] = x_scr[half:, :] + a_scr[half:, :] * h_mid
        h_last = x2 + a2 * h_mid
    else:
        def step(t, h):
            sl = pl.ds(pl.multiple_of(t * stride, stride), stride)
            h = a_scr[sl, :] * h + x_scr[sl, :]
            x_scr[sl, :] = h
            return h

        h_last = lax.fori_loop(0, n_steps, step, h_scr[...], unroll=min(8, n_steps))
    h_scr[...] = h_last
    ho_ref[...] = h_last

    b_lru = _dot((x_scr[...] * gelu_scr[...]).astype(_BF16), wbr_ref[...])
    m_ref[...] = m_ref[...] * b_lru


def _const_spec(shape, layer):
    nd = len(shape)
    return pl.BlockSpec((None,) + tuple(shape[1:]), lambda *_: (layer,) + (0,) * (nd - 1),
                        pipeline_mode=pl.Buffered(1))


def _pre_call(x, conv_state, h_state, w, layer, *, stride, rows, is_prompt, kv_stack=None):
    b, r, d = x.shape
    assert r % rows == 0 and rows % stride == 0 and rows >= (CONV_W - 1) * stride
    hist = (CONV_W - 1) * stride
    pad = -(-hist // SUBLANES) * SUBLANES
    grid = (b, r // rows)
    depth = w['g_mix'].shape[0]
    row_spec = lambda c: pl.BlockSpec((None, rows, c), lambda bi, i: (bi, i, 0))
    state_spec = lambda c: pl.BlockSpec((None, c, d), lambda bi, i: (bi, 0, 0))
    in_specs = [
        row_spec(d), state_spec(hist), state_spec(stride),
        _const_spec(w['g_mix'].shape, layer), _const_spec(w['w_in'].shape, layer),
        _const_spec(w['conv_w'].shape, layer), _const_spec(w['conv_b'].shape, layer),
        _const_spec(w['w_ai'].shape, layer), _const_spec(w['b_a'].shape, layer),
        _const_spec(w['b_i'].shape, layer), _const_spec(w['lru_lambda'].shape, layer),
        _const_spec(w['w_br_lru'].shape, layer),
    ]
    args = [x, conv_state, h_state, w['g_mix'], w['w_in'], w['conv_w'], w['conv_b'], w['w_ai'], w['b_a'],
            w['b_i'], w['lru_lambda'], w['w_br_lru']]
    aliases = {}
    if kv_stack is not None:
        aliases = {len(args): 1, len(args) + 1: 2}
        args += list(kv_stack)
        in_specs += [pl.BlockSpec(memory_space=pl.ANY)] * 2
    dq = N_STREAMS * LANES
    dk = N_KV_HEADS * 2 * D_HEAD
    dv = N_KV_HEADS * D_VHEAD
    head_spec = lambda nh, c: pl.BlockSpec((None, nh, rows, c), lambda bi, i: (bi, 0, i, 0))
    if is_prompt:
        kv_shape = jax.ShapeDtypeStruct((depth, b, r * N_KV_HEADS, LANES), _F32)
        kv_spec = pl.BlockSpec((None, None, rows * N_KV_HEADS, LANES), lambda bi, i: (layer, bi, i, 0))
        out_shape = [jax.ShapeDtypeStruct((b, N_STREAMS, r, LANES), _BF16), kv_shape, kv_shape]
        out_specs = [head_spec(N_STREAMS, LANES), kv_spec, kv_spec]
    else:
        out_shape = [jax.ShapeDtypeStruct((b, r, dq), _BF16), jax.ShapeDtypeStruct((b, r, dk), _F32),
                     jax.ShapeDtypeStruct((b, r, dv), _F32)]
        out_specs = [row_spec(dq), row_spec(dk), row_spec(dv)]
    out_shape += [jax.ShapeDtypeStruct((b, r, d), _F32), jax.ShapeDtypeStruct((b, r, d), _F32),
                  jax.ShapeDtypeStruct((b, hist, d), _F32), jax.ShapeDtypeStruct((b, stride, d), _F32)]
    out_specs += [row_spec(d), row_spec(d), state_spec(hist), state_spec(stride)]
    if is_prompt:
        out_shape += [jax.ShapeDtypeStruct((b, 2 * N_KV_HEADS, r, LANES), _BF16),
                      jax.ShapeDtypeStruct((b, N_KV_HEADS, r, 2 * D_VHEAD), _BF16)]
        out_specs += [head_spec(2 * N_KV_HEADS, LANES), head_spec(N_KV_HEADS, 2 * D_VHEAD)]
    kern = functools.partial(_pre_kernel, stride=stride, rows=rows, pad=pad, is_prompt=is_prompt,
                             n_alias=len(aliases))
    return pl.pallas_call(
        kern, grid=grid, in_specs=in_specs, out_specs=out_specs, out_shape=out_shape,
        input_output_aliases=aliases,
        scratch_shapes=[pltpu.VMEM((pad + rows, d), _F32), pltpu.VMEM((rows, d), _F32),
                        pltpu.VMEM((rows, d), _F32), pltpu.VMEM((rows, d), _F32),
                        pltpu.VMEM((stride, d), _F32)],
        compiler_params=pltpu.CompilerParams(dimension_semantics=("arbitrary", "arbitrary"),
                                             vmem_limit_bytes=VMEM_LIMIT_BYTES),
        name="pre_prompt" if is_prompt else "pre_sample",
    )(*args)


def _attn_prompt_kernel(qt_ref, kt_ref, q_ref, k_ref, v_ref, lq1_ref, lk1_ref, lq2_ref, lk2_ref, gs_ref,
                        o_ref, m_scr, acc_scr, *, lam_init, tq, tk):
    s_idx = pl.program_id(1)
    qi = qt_ref[s_idx]
    kj = kt_ref[s_idx]
    rows = GROUP * tq
    n_pairs = 2 * N_KV_HEADS

    @pl.when(kj == 0)
    def _():
        m_scr[...] = jnp.full(m_scr.shape, NEG_BIG, _F32)
        acc_scr[...] = jnp.zeros(acc_scr.shape, _F32)

    def update(n_cols, masked):
        if masked:
            row = lax.broadcasted_iota(jnp.int32, (rows, tq), 0) % tq
            col = lax.broadcasted_iota(jnp.int32, (rows, tq), 1)
            keep = col <= row
        def scores(pair):
            q = q_ref[GROUP * pair:GROUP * (pair + 1)].reshape(rows, LANES)
            return _dot_nt(q, k_ref[pair, 0:n_cols, :])

        s_next = scores(0)
        for pair in range(n_pairs):
            s = s_next
            if pair + 1 < n_pairs:
                s_next = scores(pair + 1)
            if masked:
                diag = jnp.where(keep, s[:, n_cols - tq:], NEG_BIG)
                s = diag if n_cols == tq else jnp.concatenate([s[:, :n_cols - tq], diag], axis=1)
            m_prev = m_scr[pair]
            m_new = jnp.maximum(m_prev, jnp.max(s, axis=1, keepdims=True))
            alpha = jnp.exp2(m_prev - m_new)
            p = jnp.exp2(s - jnp.concatenate([m_new] * (n_cols // LANES), axis=1))
            pv = _dot(p.astype(_BF16), v_ref[pair // 2, 0:n_cols, :])
            acc_scr[pair] = jnp.concatenate([alpha, alpha], axis=1) * acc_scr[pair] + pv
            m_scr[pair] = m_new

    n_sub = tk // tq
    is_last = kj == qi // n_sub

    @pl.when(jnp.logical_not(is_last))
    def _():
        update(tk, False)

    for r in range(n_sub):
        @pl.when(jnp.logical_and(is_last, qi % n_sub == r))
        def _():
            update((r + 1) * tq, True)

    @pl.when(is_last)
    def _():
        lam = _lam_value(lq1_ref[...], lk1_ref[...], lq2_ref[...], lk2_ref[...], lam_init)
        for hg in range(N_HEADS):
            h, g = divmod(hg, GROUP)
            a0 = acc_scr[2 * h, g * tq:(g + 1) * tq, :]
            a1 = acc_scr[2 * h + 1, g * tq:(g + 1) * tq, :]
            o = a0[:, :D_VHEAD] / a0[:, D_VHEAD:] - lam * (a1[:, :D_VHEAD] / a1[:, D_VHEAD:])
            o_ref[:, hg * LANES:(hg + 1) * LANES] = (_rms(o, gs_ref[...]) * (1.0 - lam_init)).astype(o_ref.dtype)


def _attn_prompt_call(q, k, v, w, layer):
    b, _, t, _ = q.shape
    tq = min(ATTN_Q_BLOCK, t)
    tk = min(ATTN_K_BLOCK, t)
    assert t % tk == 0 and tk % tq == 0 and t <= POS_SPLIT * 256
    pairs = [(qi, kj) for qi in range(t // tq) for kj in range((qi * tq) // tk + 1)]
    qt = jnp.asarray([p[0] for p in pairs], jnp.int32)
    kt = jnp.asarray([p[1] for p in pairs], jnp.int32)
    small = lambda a: pl.BlockSpec((None,) + tuple(a.shape[1:]), lambda bi, s, qt, kt: (layer, 0, 0))
    grid_spec = pltpu.PrefetchScalarGridSpec(
        num_scalar_prefetch=2, grid=(b, len(pairs)),
        in_specs=[
            pl.BlockSpec((None, q.shape[1], tq, LANES), lambda bi, s, qt, kt: (bi, 0, qt[s], 0)),
            pl.BlockSpec((None, k.shape[1], tk, LANES), lambda bi, s, qt, kt: (bi, 0, kt[s], 0)),
            pl.BlockSpec((None, v.shape[1], tk, v.shape[-1]), lambda bi, s, qt, kt: (bi, 0, kt[s], 0)),
            small(w['lam_q1']), small(w['lam_k1']), small(w['lam_q2']), small(w['lam_k2']),
            small(w['g_subln']),
        ],
        out_specs=pl.BlockSpec((None, tq, N_HEADS * D_VHEAD), lambda bi, s, qt, kt: (bi, qt[s], 0)),
        scratch_shapes=[pltpu.VMEM((2 * N_KV_HEADS, GROUP * tq, LANES), _F32),
                        pltpu.VMEM((2 * N_KV_HEADS, GROUP * tq, 2 * D_VHEAD), _F32)],
    )
    kern = functools.partial(_attn_prompt_kernel, lam_init=_lam_init(layer), tq=tq, tk=tk)
    return pl.pallas_call(
        kern, grid_spec=grid_spec,
        out_shape=jax.ShapeDtypeStruct((b, t, N_HEADS * D_VHEAD), _BF16),
        compiler_params=pltpu.CompilerParams(dimension_semantics=("arbitrary", "arbitrary"),
                                             vmem_limit_bytes=VMEM_LIMIT_BYTES),
        name="attn_prompt",
    )(qt, kt, q, k, v, w['lam_q1'], w['lam_k1'], w['lam_q2'], w['lam_k2'], w['g_subln'])


def _attn_sample_kernel(pt_ref, q_ref, kn_ref, vn_ref, lq1_ref, lk1_ref, lq2_ref, lk2_ref, gs_ref, *rest,
                        lam_init, n_pages, page, n_new):
    k_pages = rest[:n_pages]
    v_pages = rest[n_pages:2 * n_pages]
    o_ref = rest[2 * n_pages]
    del pt_ref
    past = n_pages * page
    n_rows = q_ref.shape[0]
    rows_per_head = GROUP * 2 * n_new
    r_idx = lax.broadcasted_iota(jnp.int32, (n_rows, 1), 0)
    h_row = r_idx // rows_per_head
    hg_row = r_idx // (2 * n_new)
    t_row = r_idx % n_new
    slope = jnp.zeros((n_rows, 1), _F32)
    for hg in range(N_HEADS):
        slope = jnp.where(hg_row == hg, SLOPES[hg], slope)
    q_pos = (past + t_row).astype(_F32)

    width = page * N_KV_HEADS
    col = lax.broadcasted_iota(jnp.int32, (1, width), 1)
    valid = (col % N_KV_HEADS) == h_row
    base = jnp.where(valid, slope * ((col // N_KV_HEADS).astype(_F32) - q_pos), NEG_BIG)
    q = q_ref[...]
    s_pages = [_dot_nt(q, kp[...].astype(_BF16)) + (base + slope * float(j * page))
               for j, kp in enumerate(k_pages)]

    n_cols_new = kn_ref.shape[0]
    col_n = lax.broadcasted_iota(jnp.int32, (1, n_cols_new), 1)
    t_key = col_n // N_KV_HEADS
    valid_n = ((col_n % N_KV_HEADS) == h_row) & (t_key <= t_row)
    s_new = _dot_nt(q, kn_ref[...].astype(_BF16))
    s_new = jnp.where(valid_n, s_new - slope * (t_row - t_key).astype(_F32), NEG_BIG)

    m = jnp.max(s_new, axis=-1, keepdims=True)
    for s in s_pages:
        m = jnp.maximum(m, jnp.max(s, axis=-1, keepdims=True))
    p_new = jnp.exp(s_new - m)
    l = jnp.sum(p_new, axis=-1, keepdims=True)
    acc = _dot(p_new.astype(_BF16), vn_ref[...].astype(_BF16))
    for s, vp in zip(s_pages, v_pages):
        p = jnp.exp(s - m)
        l = l + jnp.sum(p, axis=-1, keepdims=True)
        acc = acc + _dot(p.astype(_BF16), vp[...].astype(_BF16))
    on = acc / l
    lam = _lam_value(lq1_ref[...], lk1_ref[...], lq2_ref[...], lk2_ref[...], lam_init)
    o = on - lam * pltpu.roll(on, n_rows - n_new, 0)
    o_ref[...] = (_rms(o, gs_ref[...]) * (1.0 - lam_init)).astype(o_ref.dtype)


def _attn_sample_call(q, k_new, v_new, cache_k, cache_v, page_table, w, layer):
    n, n_rows, _ = q.shape
    n_new = n_rows // (N_KV_HEADS * GROUP * 2)
    n_pages = page_table.shape[1]
    page = cache_k.shape[2] // N_KV_HEADS
    small = lambda a: pl.BlockSpec((None,) + tuple(a.shape[1:]), lambda bi, pt: (layer, 0, 0))
    per_seq = lambda a: pl.BlockSpec((None,) + tuple(a.shape[1:]), lambda bi, pt: (bi, 0, 0))
    page_spec = lambda j: pl.BlockSpec((None, None) + tuple(cache_k.shape[2:]),
                                       lambda bi, pt: (layer, pt[bi * n_pages + j], 0, 0))
    grid_spec = pltpu.PrefetchScalarGridSpec(
        num_scalar_prefetch=1, grid=(n,),
        in_specs=[per_seq(q), per_seq(k_new), per_seq(v_new),
                  small(w['lam_q1']), small(w['lam_k1']), small(w['lam_q2']), small(w['lam_k2']),
                  small(w['g_subln'])]
                 + [page_spec(j) for j in range(n_pages)] + [page_spec(j) for j in range(n_pages)],
        out_specs=pl.BlockSpec((None, n_rows, LANES), lambda bi, pt: (bi, 0, 0)),
    )
    kern = functools.partial(_attn_sample_kernel, lam_init=_lam_init(layer), n_pages=n_pages, page=page,
                             n_new=n_new)
    return pl.pallas_call(
        kern, grid_spec=grid_spec,
        out_shape=jax.ShapeDtypeStruct((n, n_rows, LANES), _BF16),
        compiler_params=pltpu.CompilerParams(dimension_semantics=("arbitrary",),
                                             vmem_limit_bytes=VMEM_LIMIT_BYTES),
        name="attn_sample",
    )(page_table.reshape(-1), q, k_new, v_new, w['lam_q1'], w['lam_k1'], w['lam_q2'], w['lam_k2'],
      w['g_subln'], *([cache_k] * n_pages), *([cache_v] * n_pages))


def _post_kernel(h_ref, m_ref, g_ref, o_ref, p_ref, wba_ref, wo_ref, gffn_ref, wg_ref, wu_ref, wd_ref,
                 gple_ref, wpg_ref, wp_ref, gfin_ref, out_ref, *, is_last):
    mix = m_ref[...] + g_ref[...] * _dot(o_ref[...], wba_ref[...])
    h = h_ref[...] + _dot(mix.astype(_BF16), wo_ref[...])
    n2 = _rms(h, gffn_ref[...]).astype(_BF16)
    act = jax.nn.silu(_dot(n2, wg_ref[...])) * _dot(n2, wu_ref[...])
    h = h + _dot(act.astype(_BF16), wd_ref[...])
    n3 = _rms(h, gple_ref[...]).astype(_BF16)
    pg = jax.nn.sigmoid(_dot(n3, wpg_ref[...]))
    h = h + _dot(p_ref[...].astype(_BF16), wp_ref[...]) * pg
    out_ref[...] = _rms(h, gfin_ref[...]) if is_last else h


def _post_call(h, m_lru, g_att, o, p, w, layer, *, is_last):
    r, d = h.shape
    rows = min(ROW_BLOCK, r)
    assert r % rows == 0
    row_spec = lambda c: pl.BlockSpec((rows, c), lambda i: (i, 0))
    names = ['w_br_att', 'w_o', 'g_ffn', 'w_gate', 'w_up', 'w_down', 'g_ple', 'w_ple_gate', 'w_ple']
    kern = functools.partial(_post_kernel, is_last=is_last)
    return pl.pallas_call(
        kern, grid=(r // rows,),
        in_specs=[row_spec(d), row_spec(d), row_spec(d), row_spec(o.shape[-1]),
                  pl.BlockSpec((None, rows, p.shape[-1]), lambda i: (layer, i, 0))]
                 + [_const_spec(w[nm].shape, layer) for nm in names]
                 + [pl.BlockSpec(w['g_final'].shape, lambda i: (0, 0), pipeline_mode=pl.Buffered(1))],
        out_specs=row_spec(d),
        out_shape=jax.ShapeDtypeStruct((r, d), _F32),
        compiler_params=pltpu.CompilerParams(dimension_semantics=("arbitrary",),
                                             vmem_limit_bytes=VMEM_LIMIT_BYTES),
        name="post",
    )(h, m_lru, g_att, o, p, *[w[nm] for nm in names], w['g_final'])


def kernel(x_prompt, x_sample, p_prompt, p_sample, cache_k, cache_v, page_table, state_conv, state_h,
           w_in, g_mix, conv_w, conv_b, w_a, b_a, w_i, b_i, lru_lambda, lam_q1, lam_k1, lam_q2, lam_k2,
           g_subln, w_br_lru, w_br_att, w_o, g_ffn, w_gate, w_up, w_down, g_ple, w_ple_gate, w_ple, g_final):
    depth, d = g_mix.shape
    bp, t, _ = x_prompt.shape
    ns, ts, _ = x_sample.shape
    hist = CONV_W - 1
    row3 = lambda a: a.reshape(depth, 1, a.shape[-1])
    w = dict(
        w_in=w_in.astype(_BF16), g_mix=row3(g_mix), conv_w=conv_w, conv_b=row3(conv_b),
        w_ai=jnp.concatenate([w_a, w_i], axis=-1).astype(_BF16), b_a=row3(b_a), b_i=row3(b_i),
        lru_lambda=row3(lru_lambda), w_br_lru=w_br_lru.astype(_BF16),
        lam_q1=row3(lam_q1), lam_k1=row3(lam_k1), lam_q2=row3(lam_q2), lam_k2=row3(lam_k2),
        g_subln=row3(g_subln), w_br_att=w_br_att.astype(_BF16), w_o=w_o.astype(_BF16), g_ffn=row3(g_ffn),
        w_gate=w_gate.astype(_BF16), w_up=w_up.astype(_BF16), w_down=w_down.astype(_BF16),
        g_ple=row3(g_ple), w_ple_gate=w_ple_gate.astype(_BF16), w_ple=w_ple.astype(_BF16),
        g_final=g_final.reshape(1, d),
    )
    ck = cache_k.reshape(cache_k.shape[0], cache_k.shape[1], -1, cache_k.shape[-1])
    cv = cache_v.reshape(cache_v.shape[0], cache_v.shape[1], -1, cache_v.shape[-1])
    dk = N_KV_HEADS * 2 * D_HEAD
    dv = N_KV_HEADS * D_VHEAD

    tm = lambda a: jnp.swapaxes(a, -3, -2)
    hp = x_prompt
    hs = tm(x_sample).reshape(1, ts * ns, d)
    cs_all = tm(state_conv).reshape(depth, 1, hist * ns, d)
    ps_all = tm(p_sample).reshape(depth, ts * ns, -1)
    zero_c = jnp.zeros((bp, hist, d), _F32)
    zero_h = jnp.zeros((bp, 1, d), _F32)
    rows_p = min(ROW_BLOCK, t)
    pp_all = p_prompt.reshape(depth, bp * t, -1)
    k_stack = jnp.zeros((depth, bp, t * N_KV_HEADS, LANES), _F32)
    v_stack = jnp.zeros((depth, bp, t * N_KV_HEADS, LANES), _F32)

    cs_p, hs_p, ks_s, vs_s, cs_s, hs_s = ([] for _ in range(6))
    for l in range(depth):
        last = l == depth - 1
        q, k_stack, v_stack, m_lru, g_att, c_out, h_out, k_rdy, v_aug = _pre_call(
            hp, zero_c, zero_h, w, l, stride=1, rows=rows_p, is_prompt=True,
            kv_stack=(k_stack, v_stack))
        o = _attn_prompt_call(q, k_rdy, v_aug, w, l)
        hp = _post_call(hp.reshape(bp * t, d), m_lru.reshape(bp * t, d), g_att.reshape(bp * t, d),
                        o.reshape(bp * t, -1), pp_all, w, l,
                        is_last=last).reshape(bp, t, d)
        cs_p.append(c_out)
        hs_p.append(h_out.reshape(bp, d))
        q, k, v, m_lru, g_att, c_out, h_out = _pre_call(
            hs, cs_all[l], state_h[l][None], w, l, stride=ns, rows=ts * ns, is_prompt=False)
        qs = q.reshape(ts, ns, N_KV_HEADS, GROUP, 2, LANES).transpose(1, 2, 3, 4, 0, 5)
        qs = qs.reshape(ns, N_STREAMS * ts, LANES)
        k_sm = tm(k.reshape(ts, ns, dk))
        v_sm = tm(v.reshape(ts, ns, dv))
        o = _attn_sample_call(qs, k_sm.reshape(ns, ts * N_KV_HEADS, -1), v_sm.reshape(ns, ts * N_KV_HEADS, -1),
                              ck, cv, page_table, w, l)
        o = o.reshape(ns, N_KV_HEADS, GROUP, 2, ts, D_VHEAD)[:, :, :, 0]
        o = o.transpose(3, 0, 1, 2, 4).reshape(ts * ns, -1)
        hs = _post_call(hs.reshape(ts * ns, d), m_lru.reshape(ts * ns, d), g_att.reshape(ts * ns, d),
                        o, ps_all, w, l, is_last=last).reshape(1, ts * ns, d)
        ks_s.append(k_sm.reshape(ns, ts, N_KV_HEADS, 2 * D_HEAD))
        vs_s.append(v_sm.reshape(ns, ts, N_KV_HEADS, D_VHEAD))
        cs_s.append(tm(c_out.reshape(hist, ns, d)))
        hs_s.append(h_out.reshape(ns, d))

    y_sample = tm(hs.reshape(ts, ns, d))
    k_prompt = k_stack.reshape(depth, bp, t, N_KV_HEADS, 2 * D_HEAD)
    v_prompt = v_stack.reshape(depth, bp, t, N_KV_HEADS, D_VHEAD)
    return (hp, y_sample, k_prompt, v_prompt, jnp.stack(cs_p), jnp.stack(hs_p),
            jnp.stack(ks_s), jnp.stack(vs_s), jnp.stack(cs_s), jnp.stack(hs_s))
```

```python
import functools
import math

import jax
import jax.numpy as jnp
import numpy as np
from jax import lax
from jax.experimental import pallas as pl
from jax.experimental.pallas import tpu as pltpu

_F32 = jnp.float32
_BF16 = jnp.bfloat16

N_HEADS = 8
N_KV_HEADS = 4
GROUP = N_HEADS // N_KV_HEADS
D_HEAD = 64
D_VHEAD = 2 * D_HEAD
N_STREAMS = N_HEADS * 2
N_LRU_BLOCKS = 8
CONV_W = 4
LRU_C = 8.0
EPS = 1e-6
SLOPES = tuple(2.0 ** (-8.0 * (i + 1) / N_HEADS) for i in range(N_HEADS))
NEG_BIG = -1e30

LANES = 128
SUBLANES = 8
VMEM_LIMIT_BYTES = 56 * 1024 * 1024

ROW_BLOCK = 512
ATTN_Q_BLOCK = 512
ATTN_K_BLOCK = 1024
POS_SPLIT = 64
LOG2E = math.log2(math.e)
N_PIECES = 3


def _bf16_pieces(value):
    pieces, rest = [], float(value)
    for _ in range(N_PIECES):
        piece = float(np.asarray(rest, np.float32).astype(_BF16).astype(np.float64))
        pieces.append(piece)
        rest -= piece
    return tuple(pieces)


SLOPE_PIECES = tuple(_bf16_pieces(LOG2E * s) for s in SLOPES)


def _lam_init(layer):
    return 0.8 - 0.6 * math.exp(-0.3 * layer)


def _rms(x, g):
    return x * lax.rsqrt(jnp.mean(x * x, axis=-1, keepdims=True) + EPS) * g


def _dot(a, b):
    return jnp.dot(a, b, preferred_element_type=_F32)


def _dot_nt(a, b):
    return lax.dot_general(a, b, (((1,), (1,)), ((), ())), preferred_element_type=_F32)


def _softplus(x):
    return jnp.maximum(x, 0.0) + jnp.log1p(jnp.exp(-jnp.abs(x)))


def _lam_value(q1, k1, q2, k2, lam_init):
    s1 = jnp.sum(q1 * k1, axis=-1, keepdims=True)
    s2 = jnp.sum(q2 * k2, axis=-1, keepdims=True)
    return jnp.exp(s1) - jnp.exp(s2) + lam_init


def _pre_kernel(x_ref, cs_ref, hs_ref, gmix_ref, w_in_ref, cw_ref, cb_ref, wai_ref, ba_ref, bi_ref,
                lam_ref, wbr_ref, *rest, stride, rows, pad, is_prompt, n_alias):
    rest = rest[n_alias:]
    if is_prompt:
        (q_ref, k_ref, v_ref, m_ref, g_ref, co_ref, ho_ref, kr_ref, va_ref,
         up_scr, a_scr, x_scr, gelu_scr, h_scr) = rest
    else:
        (q_ref, k_ref, v_ref, m_ref, g_ref, co_ref, ho_ref,
         up_scr, a_scr, x_scr, gelu_scr, h_scr) = rest
    d = x_ref.shape[-1]
    i = pl.program_id(1)
    hist = (CONV_W - 1) * stride

    n = _rms(x_ref[...], gmix_ref[...]).astype(_BF16)

    @pl.when(i == 0)
    def _():
        up_scr[pl.ds(pad - hist, hist), :] = cs_ref[...]
        h_scr[...] = hs_ref[...]

    o_q = 2 * d
    o_k = o_q + N_HEADS * 2 * D_HEAD
    o_v = o_k + N_KV_HEADS * 2 * D_HEAD
    o_g = o_v + N_KV_HEADS * D_VHEAD
    cw = 2 * LANES
    lane = lax.broadcasted_iota(jnp.int32, (1, LANES), 1)
    low = lane < D_HEAD
    proj = lambda off: _dot(n, w_in_ref[:, off:off + cw])
    if is_prompt:
        pos = i * rows + lax.broadcasted_iota(jnp.int32, (rows, 1), 0)
        hi = (pos // POS_SPLIT).astype(_F32)
        lo = (pos % POS_SPLIT).astype(_F32)
        in_hi = (lane >= D_HEAD) & (lane < D_HEAD + N_PIECES)
        in_lo = (lane >= D_HEAD + N_PIECES) & (lane < D_HEAD + 2 * N_PIECES)
        k_extra = jnp.where(in_hi, hi, jnp.where(in_lo, lo, 0.0))
        q_scale = LOG2E * D_HEAD ** -0.5
    else:
        q_scale = D_HEAD ** -0.5

    def q_extra_lanes(hg):
        extra = jnp.zeros((1, LANES), _F32)
        for j, piece in enumerate(SLOPE_PIECES[hg]):
            extra = jnp.where(lane == D_HEAD + j, piece * POS_SPLIT, extra)
            extra = jnp.where(lane == D_HEAD + N_PIECES + j, piece, extra)
        return extra

    def gelu_task(c):
        gelu_scr[:, c * cw:(c + 1) * cw] = jax.nn.gelu(proj(d + c * cw))

    def q_task(c):
        qz = proj(o_q + c * cw) * q_scale
        for hg in range(2 * c, 2 * c + 2):
            chunk = qz[:, (hg - 2 * c) * LANES:(hg - 2 * c + 1) * LANES]
            if is_prompt:
                h, g = divmod(hg, GROUP)
                q_extra = q_extra_lanes(hg)
                q_ref[(2 * h) * GROUP + g] = jnp.where(low, chunk, q_extra).astype(_BF16)
                q_ref[(2 * h + 1) * GROUP + g] = jnp.where(
                    low, pltpu.roll(chunk, D_HEAD, 1), q_extra).astype(_BF16)
            else:
                q_ref[:, (2 * hg) * LANES:(2 * hg + 1) * LANES] = jnp.where(low, chunk, 0.0).astype(_BF16)
                q_ref[:, (2 * hg + 1) * LANES:(2 * hg + 2) * LANES] = jnp.where(low, 0.0, chunk).astype(_BF16)

    def k_task(c):
        kz = proj(o_k + c * cw)
        for h in range(2 * c, 2 * c + 2):
            chunk = kz[:, (h - 2 * c) * LANES:(h - 2 * c + 1) * LANES]
            if is_prompt:
                k_ref[pl.ds(h, rows, stride=N_KV_HEADS), :] = chunk
                kr_ref[2 * h] = jnp.where(low, chunk, k_extra).astype(_BF16)
                kr_ref[2 * h + 1] = jnp.where(low, pltpu.roll(chunk, D_HEAD, 1), k_extra).astype(_BF16)
            else:
                k_ref[:, h * LANES:(h + 1) * LANES] = chunk

    def v_task(c):
        vz = proj(o_v + c * cw)
        for h in range(2 * c, 2 * c + 2):
            chunk = vz[:, (h - 2 * c) * LANES:(h - 2 * c + 1) * LANES]
            if is_prompt:
                v_ref[pl.ds(h, rows, stride=N_KV_HEADS), :] = chunk
                va_ref[h, :, 0:D_VHEAD] = chunk.astype(_BF16)
                va_ref[h, :, D_VHEAD:2 * D_VHEAD] = jnp.ones((rows, LANES), _BF16)
            else:
                v_ref[:, h * LANES:(h + 1) * LANES] = chunk

    def gate_task(ref, off, c):
        ref[:, c * cw:(c + 1) * cw] = jax.nn.sigmoid(proj(off + c * cw))

    n_chunks = d // cw
    tasks = ([functools.partial(gelu_task, c) for c in range(n_chunks)]
             + [functools.partial(q_task, c) for c in range(N_HEADS // 2)]
             + [functools.partial(k_task, c) for c in range(N_KV_HEADS // 2)]
             + [functools.partial(v_task, c) for c in range(N_KV_HEADS // 2)]
             + [functools.partial(gate_task, m_ref, o_g, c) for c in range(n_chunks)]
             + [functools.partial(gate_task, g_ref, o_g + d, c) for c in range(n_chunks)])
    n_slots = N_LRU_BLOCKS + 1
    bounds = [len(tasks) * s // n_slots for s in range(n_slots + 1)]

    def run_tasks(slot):
        for task in tasks[bounds[slot]:bounds[slot + 1]]:
            task()

    up_scr[pl.ds(pad, rows), :] = _dot(n, w_in_ref[:, 0:d])
    run_tasks(0)
    uc = up_scr[pl.ds(pad - hist, rows), :] * cw_ref[0:1, :]
    for j in range(1, CONV_W):
        uc = uc + up_scr[pl.ds(pad - hist + j * stride, rows), :] * cw_ref[j:j + 1, :]
    uc = uc + cb_ref[...]
    tail = up_scr[pl.ds(pad + rows - hist, hist), :]
    co_ref[...] = tail
    up_scr[pl.ds(pad - hist, hist), :] = tail

    ucb = uc.astype(_BF16)
    blk = d // N_LRU_BLOCKS
    for nb in range(N_LRU_BLOCKS):
        sl = slice(nb * blk, (nb + 1) * blk)
        gates = _dot(ucb[:, sl], wai_ref[nb])
        r = jax.nn.sigmoid(gates[:, :blk] + ba_ref[:, sl])
        ig = jax.nn.sigmoid(gates[:, blk:] + bi_ref[:, sl])
        a = jnp.exp(-LRU_C * r * _softplus(-lam_ref[:, sl]))
        a_scr[:, sl] = a
        x_scr[:, sl] = jnp.sqrt(1.0 - a * a) * (ig * uc[:, sl])
        run_tasks(nb + 1)

    n_steps = rows // stride
    if stride == 1 and n_steps % 16 == 0:
        half = n_steps // 2

        def step2(t, carry):
            h1, x2, a2 = carry
            r1 = pl.ds(t, 1)
            r2 = pl.ds(half + t, 1)
            h1 = a_scr[r1, :] * h1 + x_scr[r1, :]
            a_row = a_scr[r2, :]
            x2 = a_row * x2 + x_scr[r2, :]
            a2 = a_row * a2
            x_scr[r1, :] = h1
            x_scr[r2, :] = x2
            a_scr[r2, :] = a2
            return h1, x2, a2

        h_mid, x2, a2 = lax.fori_loop(0, half, step2,
                                      (h_scr[...], jnp.zeros((1, d), _F32), jnp.ones((1, d), _F32)), unroll=8)
        x_scr[half:, :] = x_scr[half:, :] + a_scr[half:, :] * h_mid
        h_last = x2 + a2 * h_mid
    else:
        def step(t, h):
            sl = pl.ds(pl.multiple_of(t * stride, stride), stride)
            h = a_scr[sl, :] * h + x_scr[sl, :]
            x_scr[sl, :] = h
            return h

        h_last = lax.fori_loop(0, n_steps, step, h_scr[...], unroll=min(8, n_steps))
    h_scr[...] = h_last
    ho_ref[...] = h_last

    b_lru = _dot((x_scr[...] * gelu_scr[...]).astype(_BF16), wbr_ref[...])
    m_ref[...] = m_ref[...] * b_lru


def _const_spec(shape, layer):
    nd = len(shape)
    return pl.BlockSpec((None,) + tuple(shape[1:]), lambda *_: (layer,) + (0,) * (nd - 1),
                        pipeline_mode=pl.Buffered(1))


def _pre_call(x, conv_state, h_state, w, layer, *, stride, rows, is_prompt, kv_stack=None):
    b, r, d = x.shape
    assert r % rows == 0 and rows % stride == 0 and rows >= (CONV_W - 1) * stride
    hist = (CONV_W - 1) * stride
    pad = -(-hist // SUBLANES) * SUBLANES
    grid = (b, r // rows)
    depth = w['g_mix'].shape[0]
    row_spec = lambda c: pl.BlockSpec((None, rows, c), lambda bi, i: (bi, i, 0))
    state_spec = lambda c: pl.BlockSpec((None, c, d), lambda bi, i: (bi, 0, 0))
    in_specs = [
        row_spec(d), state_spec(hist), state_spec(stride),
        _const_spec(w['g_mix'].shape, layer), _const_spec(w['w_in'].shape, layer),
        _const_spec(w['conv_w'].shape, layer), _const_spec(w['conv_b'].shape, layer),
        _const_spec(w['w_ai'].shape, layer), _const_spec(w['b_a'].shape, layer),
        _const_spec(w['b_i'].shape, layer), _const_spec(w['lru_lambda'].shape, layer),
        _const_spec(w['w_br_lru'].shape, layer),
    ]
    args = [x, conv_state, h_state, w['g_mix'], w['w_in'], w['conv_w'], w['conv_b'], w['w_ai'], w['b_a'],
            w['b_i'], w['lru_lambda'], w['w_br_lru']]
    aliases = {}
    if kv_stack is not None:
        aliases = {len(args): 1, len(args) + 1: 2}
        args += list(kv_stack)
        in_specs += [pl.BlockSpec(memory_space=pl.ANY)] * 2
    dq = N_STREAMS * LANES
    dk = N_KV_HEADS * 2 * D_HEAD
    dv = N_KV_HEADS * D_VHEAD
    head_spec = lambda nh, c: pl.BlockSpec((None, nh, rows, c), lambda bi, i: (bi, 0, i, 0))
    if is_prompt:
        kv_shape = jax.ShapeDtypeStruct((depth, b, r * N_KV_HEADS, LANES), _F32)
        kv_spec = pl.BlockSpec((None, None, rows * N_KV_HEADS, LANES), lambda bi, i: (layer, bi, i, 0))
        out_shape = [jax.ShapeDtypeStruct((b, N_STREAMS, r, LANES), _BF16), kv_shape, kv_shape]
        out_specs = [head_spec(N_STREAMS, LANES), kv_spec, kv_spec]
    else:
        out_shape = [jax.ShapeDtypeStruct((b, r, dq), _BF16), jax.ShapeDtypeStruct((b, r, dk), _F32),
                     jax.ShapeDtypeStruct((b, r, dv), _F32)]
        out_specs = [row_spec(dq), row_spec(dk), row_spec(dv)]
    out_shape += [jax.ShapeDtypeStruct((b, r, d), _F32), jax.ShapeDtypeStruct((b, r, d), _F32),
                  jax.ShapeDtypeStruct((b, hist, d), _F32), jax.ShapeDtypeStruct((b, stride, d), _F32)]
    out_specs += [row_spec(d), row_spec(d), state_spec(hist), state_spec(stride)]
    if is_prompt:
        out_shape += [jax.ShapeDtypeStruct((b, 2 * N_KV_HEADS, r, LANES), _BF16),
                      jax.ShapeDtypeStruct((b, N_KV_HEADS, r, 2 * D_VHEAD), _BF16)]
        out_specs += [head_spec(2 * N_KV_HEADS, LANES), head_spec(N_KV_HEADS, 2 * D_VHEAD)]
    kern = functools.partial(_pre_kernel, stride=stride, rows=rows, pad=pad, is_prompt=is_prompt,
                             n_alias=len(aliases))
    return pl.pallas_call(
        kern, grid=grid, in_specs=in_specs, out_specs=out_specs, out_shape=out_shape,
        input_output_aliases=aliases,
        scratch_shapes=[pltpu.VMEM((pad + rows, d), _F32), pltpu.VMEM((rows, d), _F32),
                        pltpu.VMEM((rows, d), _F32), pltpu.VMEM((rows, d), _F32),
                        pltpu.VMEM((stride, d), _F32)],
        compiler_params=pltpu.CompilerParams(dimension_semantics=("arbitrary", "arbitrary"),
                                             vmem_limit_bytes=VMEM_LIMIT_BYTES),
        name="pre_prompt" if is_prompt else "pre_sample",
    )(*args)


def _attn_prompt_kernel(qt_ref, kt_ref, q_ref, k_ref, v_ref, lq1_ref, lk1_ref, lq2_ref, lk2_ref, gs_ref,
                        o_ref, m_scr, acc_scr, *, lam_init, tq, tk):
    s_idx = pl.program_id(1)
    qi = qt_ref[s_idx]
    kj = kt_ref[s_idx]
    rows = GROUP * tq
    n_pairs = 2 * N_KV_HEADS

    @pl.when(kj == 0)
    def _():
        m_scr[...] = jnp.full(m_scr.shape, NEG_BIG, _F32)
        acc_scr[...] = jnp.zeros(acc_scr.shape, _F32)

    def update(n_cols, masked):
        if masked:
            row = lax.broadcasted_iota(jnp.int32, (rows, tq), 0) % tq
            col = lax.broadcasted_iota(jnp.int32, (rows, tq), 1)
            keep = col <= row
        def scores(pair):
            q = q_ref[GROUP * pair:GROUP * (pair + 1)].reshape(rows, LANES)
            return _dot_nt(q, k_ref[pair, 0:n_cols, :])

        s_next = scores(0)
        for pair in range(n_pairs):
            s = s_next
            if pair + 1 < n_pairs:
                s_next = scores(pair + 1)
            if masked:
                diag = jnp.where(keep, s[:, n_cols - tq:], NEG_BIG)
                s = diag if n_cols == tq else jnp.concatenate([s[:, :n_cols - tq], diag], axis=1)
            m_prev = m_scr[pair]
            m_new = jnp.maximum(m_prev, jnp.max(s, axis=1, keepdims=True))
            alpha = jnp.exp2(m_prev - m_new)
            p = jnp.exp2(s - jnp.concatenate([m_new] * (n_cols // LANES), axis=1))
            pv = _dot(p.astype(_BF16), v_ref[pair // 2, 0:n_cols, :])
            acc_scr[pair] = jnp.concatenate([alpha, alpha], axis=1) * acc_scr[pair] + pv
            m_scr[pair] = m_new

    n_sub = tk // tq
    is_last = kj == qi // n_sub

    @pl.when(jnp.logical_not(is_last))
    def _():
        update(tk, False)

    for r in range(n_sub):
        @pl.when(jnp.logical_and(is_last, qi % n_sub == r))
        def _():
            update((r + 1) * tq, True)

    @pl.when(is_last)
    def _():
        lam = _lam_value(lq1_ref[...], lk1_ref[...], lq2_ref[...], lk2_ref[...], lam_init)
        for hg in range(N_HEADS):
            h, g = divmod(hg, GROUP)
            a0 = acc_scr[2 * h, g * tq:(g + 1) * tq, :]
            a1 = acc_scr[2 * h + 1, g * tq:(g + 1) * tq, :]
            o = a0[:, :D_VHEAD] / a0[:, D_VHEAD:] - lam * (a1[:, :D_VHEAD] / a1[:, D_VHEAD:])
            o_ref[:, hg * LANES:(hg + 1) * LANES] = (_rms(o, gs_ref[...]) * (1.0 - lam_init)).astype(o_ref.dtype)


def _attn_prompt_call(q, k, v, w, layer):
    b, _, t, _ = q.shape
    tq = min(ATTN_Q_BLOCK, t)
    tk = min(ATTN_K_BLOCK, t)
    assert t % tk == 0 and tk % tq == 0 and t <= POS_SPLIT * 256
    pairs = [(qi, kj) for qi in range(t // tq) for kj in range((qi * tq) // tk + 1)]
    qt = jnp.asarray([p[0] for p in pairs], jnp.int32)
    kt = jnp.asarray([p[1] for p in pairs], jnp.int32)
    small = lambda a: pl.BlockSpec((None,) + tuple(a.shape[1:]), lambda bi, s, qt, kt: (layer, 0, 0))
    grid_spec = pltpu.PrefetchScalarGridSpec(
        num_scalar_prefetch=2, grid=(b, len(pairs)),
        in_specs=[
            pl.BlockSpec((None, q.shape[1], tq, LANES), lambda bi, s, qt, kt: (bi, 0, qt[s], 0)),
            pl.BlockSpec((None, k.shape[1], tk, LANES), lambda bi, s, qt, kt: (bi, 0, kt[s], 0)),
            pl.BlockSpec((None, v.shape[1], tk, v.shape[-1]), lambda bi, s, qt, kt: (bi, 0, kt[s], 0)),
            small(w['lam_q1']), small(w['lam_k1']), small(w['lam_q2']), small(w['lam_k2']),
            small(w['g_subln']),
        ],
        out_specs=pl.BlockSpec((None, tq, N_HEADS * D_VHEAD), lambda bi, s, qt, kt: (bi, qt[s], 0)),
        scratch_shapes=[pltpu.VMEM((2 * N_KV_HEADS, GROUP * tq, LANES), _F32),
                        pltpu.VMEM((2 * N_KV_HEADS, GROUP * tq, 2 * D_VHEAD), _F32)],
    )
    kern = functools.partial(_attn_prompt_kernel, lam_init=_lam_init(layer), tq=tq, tk=tk)
    return pl.pallas_call(
        kern, grid_spec=grid_spec,
        out_shape=jax.ShapeDtypeStruct((b, t, N_HEADS * D_VHEAD), _BF16),
        compiler_params=pltpu.CompilerParams(dimension_semantics=("arbitrary", "arbitrary"),
                                             vmem_limit_bytes=VMEM_LIMIT_BYTES),
        name="attn_prompt",
    )(qt, kt, q, k, v, w['lam_q1'], w['lam_k1'], w['lam_q2'], w['lam_k2'], w['g_subln'])


def _attn_sample_kernel(pt_ref, q_ref, kn_ref, vn_ref, lq1_ref, lk1_ref, lq2_ref, lk2_ref, gs_ref, *rest,
                        lam_init, n_pages, page, n_new):
    k_pages = rest[:n_pages]
    v_pages = rest[n_pages:2 * n_pages]
    o_ref = rest[2 * n_pages]
    del pt_ref
    past = n_pages * page
    n_rows = q_ref.shape[0]
    rows_per_head = GROUP * 2 * n_new
    r_idx = lax.broadcasted_iota(jnp.int32, (n_rows, 1), 0)
    h_row = r_idx // rows_per_head
    hg_row = r_idx // (2 * n_new)
    t_row = r_idx % n_new
    slope = jnp.zeros((n_rows, 1), _F32)
    for hg in range(N_HEADS):
        slope = jnp.where(hg_row == hg, SLOPES[hg], slope)
    q_pos = (past + t_row).astype(_F32)

    width = page * N_KV_HEADS
    col = lax.broadcasted_iota(jnp.int32, (1, width), 1)
    valid = (col % N_KV_HEADS) == h_row
    base = jnp.where(valid, slope * ((col // N_KV_HEADS).astype(_F32) - q_pos), NEG_BIG)
    q = q_ref[...]
    s_pages = [_dot_nt(q, kp[...].astype(_BF16)) + (base + slope * float(j * page))
               for j, kp in enumerate(k_pages)]

    n_cols_new = kn_ref.shape[0]
    col_n = lax.broadcasted_iota(jnp.int32, (1, n_cols_new), 1)
    t_key = col_n // N_KV_HEADS
    valid_n = ((col_n % N_KV_HEADS) == h_row) & (t_key <= t_row)
    s_new = _dot_nt(q, kn_ref[...].astype(_BF16))
    s_new = jnp.where(valid_n, s_new - slope * (t_row - t_key).astype(_F32), NEG_BIG)

    m = jnp.max(s_new, axis=-1, keepdims=True)
    for s in s_pages:
        m = jnp.maximum(m, jnp.max(s, axis=-1, keepdims=True))
    p_new = jnp.exp(s_new - m)
    l = jnp.sum(p_new, axis=-1, keepdims=True)
    acc = _dot(p_new.astype(_BF16), vn_ref[...].astype(_BF16))
    for s, vp in zip(s_pages, v_pages):
        p = jnp.exp(s - m)
        l = l + jnp.sum(p, axis=-1, keepdims=True)
        acc = acc + _dot(p.astype(_BF16), vp[...].astype(_BF16))
    on = acc / l
    lam = _lam_value(lq1_ref[...], lk1_ref[...], lq2_ref[...], lk2_ref[...], lam_init)
    o = on - lam * pltpu.roll(on, n_rows - n_new, 0)
    o_ref[...] = (_rms(o, gs_ref[...]) * (1.0 - lam_init)).astype(o_ref.dtype)


def _attn_sample_call(q, k_new, v_new, cache_k, cache_v, page_table, w, layer):
    n, n_rows, _ = q.shape
    n_new = n_rows // (N_KV_HEADS * GROUP * 2)
    n_pages = page_table.shape[1]
    page = cache_k.shape[2] // N_KV_HEADS
    small = lambda a: pl.BlockSpec((None,) + tuple(a.shape[1:]), lambda bi, pt: (layer, 0, 0))
    per_seq = lambda a: pl.BlockSpec((None,) + tuple(a.shape[1:]), lambda bi, pt: (bi, 0, 0))
    page_spec = lambda j: pl.BlockSpec((None, None) + tuple(cache_k.shape[2:]),
                                       lambda bi, pt: (layer, pt[bi * n_pages + j], 0, 0))
    grid_spec = pltpu.PrefetchScalarGridSpec(
        num_scalar_prefetch=1, grid=(n,),
        in_specs=[per_seq(q), per_seq(k_new), per_seq(v_new),
                  small(w['lam_q1']), small(w['lam_k1']), small(w['lam_q2']), small(w['lam_k2']),
                  small(w['g_subln'])]
                 + [page_spec(j) for j in range(n_pages)] + [page_spec(j) for j in range(n_pages)],
        out_specs=pl.BlockSpec((None, n_rows, LANES), lambda bi, pt: (bi, 0, 0)),
    )
    kern = functools.partial(_attn_sample_kernel, lam_init=_lam_init(layer), n_pages=n_pages, page=page,
                             n_new=n_new)
    return pl.pallas_call(
        kern, grid_spec=grid_spec,
        out_shape=jax.ShapeDtypeStruct((n, n_rows, LANES), _BF16),
        compiler_params=pltpu.CompilerParams(dimension_semantics=("arbitrary",),
                                             vmem_limit_bytes=VMEM_LIMIT_BYTES),
        name="attn_sample",
    )(page_table.reshape(-1), q, k_new, v_new, w['lam_q1'], w['lam_k1'], w['lam_q2'], w['lam_k2'],
      w['g_subln'], *([cache_k] * n_pages), *([cache_v] * n_pages))


def _post_kernel(h_ref, m_ref, g_ref, o_ref, p_ref, wba_ref, wo_ref, gffn_ref, wg_ref, wu_ref, wd_ref,
                 gple_ref, wpg_ref, wp_ref, gfin_ref, out_ref, *, is_last):
    mix = m_ref[...] + g_ref[...] * _dot(o_ref[...], wba_ref[...])
    h = h_ref[...] + _dot(mix.astype(_BF16), wo_ref[...])
    n2 = _rms(h, gffn_ref[...]).astype(_BF16)
    act = jax.nn.silu(_dot(n2, wg_ref[...])) * _dot(n2, wu_ref[...])
    h = h + _dot(act.astype(_BF16), wd_ref[...])
    n3 = _rms(h, gple_ref[...]).astype(_BF16)
    pg = jax.nn.sigmoid(_dot(n3, wpg_ref[...]))
    h = h + _dot(p_ref[...].astype(_BF16), wp_ref[...]) * pg
    out_ref[...] = _rms(h, gfin_ref[...]) if is_last else h


def _post_call(h, m_lru, g_att, o, p, w, layer, *, is_last):
    r, d = h.shape
    rows = min(ROW_BLOCK, r)
    assert r % rows == 0
    row_spec = lambda c: pl.BlockSpec((rows, c), lambda i: (i, 0))
    names = ['w_br_att', 'w_o', 'g_ffn', 'w_gate', 'w_up', 'w_down', 'g_ple', 'w_ple_gate', 'w_ple']
    kern = functools.partial(_post_kernel, is_last=is_last)
    return pl.pallas_call(
        kern, grid=(r // rows,),
        in_specs=[row_spec(d), row_spec(d), row_spec(d), row_spec(o.shape[-1]),
                  pl.BlockSpec((None, rows, p.shape[-1]), lambda i: (layer, i, 0))]
                 + [_const_spec(w[nm].shape, layer) for nm in names]
                 + [pl.BlockSpec(w['g_final'].shape, lambda i: (0, 0), pipeline_mode=pl.Buffered(1))],
        out_specs=row_spec(d),
        out_shape=jax.ShapeDtypeStruct((r, d), _F32),
        compiler_params=pltpu.CompilerParams(dimension_semantics=("arbitrary",),
                                             vmem_limit_bytes=VMEM_LIMIT_BYTES),
        name="post",
    )(h, m_lru, g_att, o, p, *[w[nm] for nm in names], w['g_final'])


def kernel(x_prompt, x_sample, p_prompt, p_sample, cache_k, cache_v, page_table, state_conv, state_h,
           w_in, g_mix, conv_w, conv_b, w_a, b_a, w_i, b_i, lru_lambda, lam_q1, lam_k1, lam_q2, lam_k2,
           g_subln, w_br_lru, w_br_att, w_o, g_ffn, w_gate, w_up, w_down, g_ple, w_ple_gate, w_ple, g_final):
    depth, d = g_mix.shape
    bp, t, _ = x_prompt.shape
    ns, ts, _ = x_sample.shape
    hist = CONV_W - 1
    row3 = lambda a: a.reshape(depth, 1, a.shape[-1])
    w = dict(
        w_in=w_in.astype(_BF16), g_mix=row3(g_mix), conv_w=conv_w, conv_b=row3(conv_b),
        w_ai=jnp.concatenate([w_a, w_i], axis=-1).astype(_BF16), b_a=row3(b_a), b_i=row3(b_i),
        lru_lambda=row3(lru_lambda), w_br_lru=w_br_lru.astype(_BF16),
        lam_q1=row3(lam_q1), lam_k1=row3(lam_k1), lam_q2=row3(lam_q2), lam_k2=row3(lam_k2),
        g_subln=row3(g_subln), w_br_att=w_br_att.astype(_BF16), w_o=w_o.astype(_BF16), g_ffn=row3(g_ffn),
        w_gate=w_gate.astype(_BF16), w_up=w_up.astype(_BF16), w_down=w_down.astype(_BF16),
        g_ple=row3(g_ple), w_ple_gate=w_ple_gate.astype(_BF16), w_ple=w_ple.astype(_BF16),
        g_final=g_final.reshape(1, d),
    )
    ck = cache_k.reshape(cache_k.shape[0], cache_k.shape[1], -1, cache_k.shape[-1])
    cv = cache_v.reshape(cache_v.shape[0], cache_v.shape[1], -1, cache_v.shape[-1])
    dk = N_KV_HEADS * 2 * D_HEAD
    dv = N_KV_HEADS * D_VHEAD

    tm = lambda a: jnp.swapaxes(a, -3, -2)
    hp = x_prompt
    hs = tm(x_sample).reshape(1, ts * ns, d)
    cs_all = tm(state_conv).reshape(depth, 1, hist * ns, d)
    ps_all = tm(p_sample).reshape(depth, ts * ns, -1)
    zero_c = jnp.zeros((bp, hist, d), _F32)
    zero_h = jnp.zeros((bp, 1, d), _F32)
    rows_p = min(ROW_BLOCK, t)
    pp_all = p_prompt.reshape(depth, bp * t, -1)
    k_stack = jnp.zeros((depth, bp, t * N_KV_HEADS, LANES), _F32)
    v_stack = jnp.zeros((depth, bp, t * N_KV_HEADS, LANES), _F32)

    cs_p, hs_p, ks_s, vs_s, cs_s, hs_s = ([] for _ in range(6))
    for l in range(depth):
        last = l == depth - 1
        q, k_stack, v_stack, m_lru, g_att, c_out, h_out, k_rdy, v_aug = _pre_call(
            hp, zero_c, zero_h, w, l, stride=1, rows=rows_p, is_prompt=True,
            kv_stack=(k_stack, v_stack))
        o = _attn_prompt_call(q, k_rdy, v_aug, w, l)
        hp = _post_call(hp.reshape(bp * t, d), m_lru.reshape(bp * t, d), g_att.reshape(bp * t, d),
                        o.reshape(bp * t, -1), pp_all, w, l,
                        is_last=last).reshape(bp, t, d)
        cs_p.append(c_out)
        hs_p.append(h_out.reshape(bp, d))
        q, k, v, m_lru, g_att, c_out, h_out = _pre_call(
            hs, cs_all[l], state_h[l][None], w, l, stride=ns, rows=ts * ns, is_prompt=False)
        qs = q.reshape(ts, ns, N_KV_HEADS, GROUP, 2, LANES).transpose(1, 2, 3, 4, 0, 5)
        qs = qs.reshape(ns, N_STREAMS * ts, LANES)
        k_sm = tm(k.reshape(ts, ns, dk))
        v_sm = tm(v.reshape(ts, ns, dv))
        o = _attn_sample_call(qs, k_sm.reshape(ns, ts * N_KV_HEADS, -1), v_sm.reshape(ns, ts * N_KV_HEADS, -1),
                              ck, cv, page_table, w, l)
        o = o.reshape(ns, N_KV_HEADS, GROUP, 2, ts, D_VHEAD)[:, :, :, 0]
        o = o.transpose(3, 0, 1, 2, 4).reshape(ts * ns, -1)
        hs = _post_call(hs.reshape(ts * ns, d), m_lru.reshape(ts * ns, d), g_att.reshape(ts * ns, d),
                        o, ps_all, w, l, is_last=last).reshape(1, ts * ns, d)
        ks_s.append(k_sm.reshape(ns, ts, N_KV_HEADS, 2 * D_HEAD))
        vs_s.append(v_sm.reshape(ns, ts, N_KV_HEADS, D_VHEAD))
        cs_s.append(tm(c_out.reshape(hist, ns, d)))
        hs_s.append(h_out.reshape(ns, d))

    y_sample = tm(hs.reshape(ts, ns, d))
    k_prompt = k_stack.reshape(depth, bp, t, N_KV_HEADS, 2 * D_HEAD)
    v_prompt = v_stack.reshape(depth, bp, t, N_KV_HEADS, D_VHEAD)
    return (hp, y_sample, k_prompt, v_prompt, jnp.stack(cs_p), jnp.stack(hs_p),
            jnp.stack(ks_s), jnp.stack(vs_s), jnp.stack(cs_s), jnp.stack(hs_s))
```

```python
import functools
import math

import jax
import jax.numpy as jnp
import numpy as np
from jax import lax
from jax.experimental import pallas as pl
from jax.experimental.pallas import tpu as pltpu

_F32 = jnp.float32
_BF16 = jnp.bfloat16

N_HEADS = 8
N_KV_HEADS = 4
GROUP = N_HEADS // N_KV_HEADS
D_HEAD = 64
D_VHEAD = 2 * D_HEAD
N_STREAMS = N_HEADS * 2
N_LRU_BLOCKS = 8
CONV_W = 4
LRU_C = 8.0
EPS = 1e-6
SLOPES = tuple(2.0 ** (-8.0 * (i + 1) / N_HEADS) for i in range(N_HEADS))
NEG_BIG = -1e30

LANES = 128
SUBLANES = 8
VMEM_LIMIT_BYTES = 56 * 1024 * 1024

ROW_BLOCK = 512
ATTN_Q_BLOCK = 512
ATTN_K_BLOCK = 1024
POS_SPLIT = 64
LOG2E = math.log2(math.e)
N_PIECES = 3


def _bf16_pieces(value):
    pieces, rest = [], float(value)
    for _ in range(N_PIECES):
        piece = float(np.asarray(rest, np.float32).astype(_BF16).astype(np.float64))
        pieces.append(piece)
        rest -= piece
    return tuple(pieces)


SLOPE_PIECES = tuple(_bf16_pieces(LOG2E * s) for s in SLOPES)


def _lam_init(layer):
    return 0.8 - 0.6 * math.exp(-0.3 * layer)


def _rms(x, g):
    return x * lax.rsqrt(jnp.mean(x * x, axis=-1, keepdims=True) + EPS) * g


def _dot(a, b):
    return jnp.dot(a, b, preferred_element_type=_F32)


def _dot_nt(a, b):
    return lax.dot_general(a, b, (((1,), (1,)), ((), ())), preferred_element_type=_F32)


def _softplus(x):
    return jnp.maximum(x, 0.0) + jnp.log1p(jnp.exp(-jnp.abs(x)))


def _lam_value(q1, k1, q2, k2, lam_init):
    s1 = jnp.sum(q1 * k1, axis=-1, keepdims=True)
    s2 = jnp.sum(q2 * k2, axis=-1, keepdims=True)
    return jnp.exp(s1) - jnp.exp(s2) + lam_init


def _pre_kernel(x_ref, cs_ref, hs_ref, gmix_ref, w_in_ref, cw_ref, cb_ref, wai_ref, ba_ref, bi_ref,
                lam_ref, wbr_ref, *rest, stride, rows, pad, is_prompt, n_alias):
    rest = rest[n_alias:]
    if is_prompt:
        (q_ref, k_ref, v_ref, m_ref, g_ref, co_ref, ho_ref, kr_ref, va_ref,
         up_scr, a_scr, x_scr, gelu_scr, h_scr) = rest
    else:
        (q_ref, k_ref, v_ref, m_ref, g_ref, co_ref, ho_ref,
         up_scr, a_scr, x_scr, gelu_scr, h_scr) = rest
    d = x_ref.shape[-1]
    i = pl.program_id(1)
    hist = (CONV_W - 1) * stride

    n = _rms(x_ref[...], gmix_ref[...]).astype(_BF16)

    @pl.when(i == 0)
    def _():
        up_scr[pl.ds(pad - hist, hist), :] = cs_ref[...]
        h_scr[...] = hs_ref[...]

    o_q = 2 * d
    o_k = o_q + N_HEADS * 2 * D_HEAD
    o_v = o_k + N_KV_HEADS * 2 * D_HEAD
    o_g = o_v + N_KV_HEADS * D_VHEAD
    cw = 2 * LANES
    lane = lax.broadcasted_iota(jnp.int32, (1, LANES), 1)
    low = lane < D_HEAD
    proj = lambda off: _dot(n, w_in_ref[:, off:off + cw])
    if is_prompt:
        pos = i * rows + lax.broadcasted_iota(jnp.int32, (rows, 1), 0)
        hi = (pos // POS_SPLIT).astype(_F32)
        lo = (pos % POS_SPLIT).astype(_F32)
        in_hi = (lane >= D_HEAD) & (lane < D_HEAD + N_PIECES)
        in_lo = (lane >= D_HEAD + N_PIECES) & (lane < D_HEAD + 2 * N_PIECES)
        k_extra = jnp.where(in_hi, hi, jnp.where(in_lo, lo, 0.0))
        q_scale = LOG2E * D_HEAD ** -0.5
    else:
        q_scale = D_HEAD ** -0.5

    def q_extra_lanes(hg):
        extra = jnp.zeros((1, LANES), _F32)
        for j, piece in enumerate(SLOPE_PIECES[hg]):
            extra = jnp.where(lane == D_HEAD + j, piece * POS_SPLIT, extra)
            extra = jnp.where(lane == D_HEAD + N_PIECES + j, piece, extra)
        return extra

    def gelu_task(c):
        gelu_scr[:, c * cw:(c + 1) * cw] = jax.nn.gelu(proj(d + c * cw))

    def q_task(c):
        qz = proj(o_q + c * cw) * q_scale
        for hg in range(2 * c, 2 * c + 2):
            chunk = qz[:, (hg - 2 * c) * LANES:(hg - 2 * c + 1) * LANES]
            if is_prompt:
                h, g = divmod(hg, GROUP)
                q_extra = q_extra_lanes(hg)
                q_ref[(2 * h) * GROUP + g] = jnp.where(low, chunk, q_extra).astype(_BF16)
                q_ref[(2 * h + 1) * GROUP + g] = jnp.where(
                    low, pltpu.roll(chunk, D_HEAD, 1), q_extra).astype(_BF16)
            else:
                q_ref[:, (2 * hg) * LANES:(2 * hg + 1) * LANES] = jnp.where(low, chunk, 0.0).astype(_BF16)
                q_ref[:, (2 * hg + 1) * LANES:(2 * hg + 2) * LANES] = jnp.where(low, 0.0, chunk).astype(_BF16)

    def k_task(c):
        kz = proj(o_k + c * cw)
        for h in range(2 * c, 2 * c + 2):
            chunk = kz[:, (h - 2 * c) * LANES:(h - 2 * c + 1) * LANES]
            if is_prompt:
                k_ref[pl.ds(h, rows, stride=N_KV_HEADS), :] = chunk
                kr_ref[2 * h] = jnp.where(low, chunk, k_extra).astype(_BF16)
                kr_ref[2 * h + 1] = jnp.where(low, pltpu.roll(chunk, D_HEAD, 1), k_extra).astype(_BF16)
            else:
                k_ref[:, h * LANES:(h + 1) * LANES] = chunk

    def v_task(c):
        vz = proj(o_v + c * cw)
        for h in range(2 * c, 2 * c + 2):
            chunk = vz[:, (h - 2 * c) * LANES:(h - 2 * c + 1) * LANES]
            if is_prompt:
                v_ref[pl.ds(h, rows, stride=N_KV_HEADS), :] = chunk
                va_ref[h, :, 0:D_VHEAD] = chunk.astype(_BF16)
                va_ref[h, :, D_VHEAD:2 * D_VHEAD] = jnp.ones((rows, LANES), _BF16)
            else:
                v_ref[:, h * LANES:(h + 1) * LANES] = chunk

    def gate_task(ref, off, c):
        ref[:, c * cw:(c + 1) * cw] = jax.nn.sigmoid(proj(off + c * cw))

    n_chunks = d // cw
    tasks = ([functools.partial(gelu_task, c) for c in range(n_chunks)]
             + [functools.partial(q_task, c) for c in range(N_HEADS // 2)]
             + [functools.partial(k_task, c) for c in range(N_KV_HEADS // 2)]
             + [functools.partial(v_task, c) for c in range(N_KV_HEADS // 2)]
             + [functools.partial(gate_task, m_ref, o_g, c) for c in range(n_chunks)]
             + [functools.partial(gate_task, g_ref, o_g + d, c) for c in range(n_chunks)])
    n_slots = N_LRU_BLOCKS + 1
    bounds = [len(tasks) * s // n_slots for s in range(n_slots + 1)]

    def run_tasks(slot):
        for task in tasks[bounds[slot]:bounds[slot + 1]]:
            task()

    up_scr[pl.ds(pad, rows), :] = _dot(n, w_in_ref[:, 0:d])
    run_tasks(0)
    uc = up_scr[pl.ds(pad - hist, rows), :] * cw_ref[0:1, :]
    for j in range(1, CONV_W):
        uc = uc + up_scr[pl.ds(pad - hist + j * stride, rows), :] * cw_ref[j:j + 1, :]
    uc = uc + cb_ref[...]
    tail = up_scr[pl.ds(pad + rows - hist, hist), :]
    co_ref[...] = tail
    up_scr[pl.ds(pad - hist, hist), :] = tail

    ucb = uc.astype(_BF16)
    blk = d // N_LRU_BLOCKS
    for nb in range(N_LRU_BLOCKS):
        sl = slice(nb * blk, (nb + 1) * blk)
        gates = _dot(ucb[:, sl], wai_ref[nb])
        r = jax.nn.sigmoid(gates[:, :blk] + ba_ref[:, sl])
        ig = jax.nn.sigmoid(gates[:, blk:] + bi_ref[:, sl])
        a = jnp.exp(-LRU_C * r * _softplus(-lam_ref[:, sl]))
        a_scr[:, sl] = a
        x_scr[:, sl] = jnp.sqrt(1.0 - a * a) * (ig * uc[:, sl])
        run_tasks(nb + 1)

    n_steps = rows // stride
    if stride == 1 and n_steps % 16 == 0:
        half = n_steps // 2

        def step2(t, carry):
            h1, x2, a2 = carry
            r1 = pl.ds(t, 1)
            r2 = pl.ds(half + t, 1)
            h1 = a_scr[r1, :] * h1 + x_scr[r1, :]
            a_row = a_scr[r2, :]
            x2 = a_row * x2 + x_scr[r2, :]
            a2 = a_row * a2
            x_scr[r1, :] = h1
            x_scr[r2, :] = x2
            a_scr[r2, :] = a2
            return h1, x2, a2

        h_mid, x2, a2 = lax.fori_loop(0, half, step2,
                                      (h_scr[...], jnp.zeros((1, d), _F32), jnp.ones((1, d), _F32)), unroll=8)
        x_scr[half:, :] = x_scr[half:, :] + a_scr[half:, :] * h_mid
        h_last = x2 + a2 * h_mid
    else:
        def step(t, h):
            sl = pl.ds(pl.multiple_of(t * stride, stride), stride)
            h = a_scr[sl, :] * h + x_scr[sl, :]
            x_scr[sl, :] = h
            return h

        h_last = lax.fori_loop(0, n_steps, step, h_scr[...], unroll=min(8, n_steps))
    h_scr[...] = h_last
    ho_ref[...] = h_last

    b_lru = _dot((x_scr[...] * gelu_scr[...]).astype(_BF16), wbr_ref[...])
    m_ref[...] = m_ref[...] * b_lru


def _const_spec(shape, layer):
    nd = len(shape)
    return pl.BlockSpec((None,) + tuple(shape[1:]), lambda *_: (layer,) + (0,) * (nd - 1),
                        pipeline_mode=pl.Buffered(1))


def _pre_call(x, conv_state, h_state, w, layer, *, stride, rows, is_prompt, kv_stack=None):
    b, r, d = x.shape
    assert r % rows == 0 and rows % stride == 0 and rows >= (CONV_W - 1) * stride
    hist = (CONV_W - 1) * stride
    pad = -(-hist // SUBLANES) * SUBLANES
    grid = (b, r // rows)
    depth = w['g_mix'].shape[0]
    row_spec = lambda c: pl.BlockSpec((None, rows, c), lambda bi, i: (bi, i, 0))
    state_spec = lambda c: pl.BlockSpec((None, c, d), lambda bi, i: (bi, 0, 0))
    in_specs = [
        row_spec(d), state_spec(hist), state_spec(stride),
        _const_spec(w['g_mix'].shape, layer), _const_spec(w['w_in'].shape, layer),
        _const_spec(w['conv_w'].shape, layer), _const_spec(w['conv_b'].shape, layer),
        _const_spec(w['w_ai'].shape, layer), _const_spec(w['b_a'].shape, layer),
        _const_spec(w['b_i'].shape, layer), _const_spec(w['lru_lambda'].shape, layer),
        _const_spec(w['w_br_lru'].shape, layer),
    ]
    args = [x, conv_state, h_state, w['g_mix'], w['w_in'], w['conv_w'], w['conv_b'], w['w_ai'], w['b_a'],
            w['b_i'], w['lru_lambda'], w['w_br_lru']]
    aliases = {}
    if kv_stack is not None:
        aliases = {len(args): 1, len(args) + 1: 2}
        args += list(kv_stack)
        in_specs += [pl.BlockSpec(memory_space=pl.ANY)] * 2
    dq = N_STREAMS * LANES
    dk = N_KV_HEADS * 2 * D_HEAD
    dv = N_KV_HEADS * D_VHEAD
    head_spec = lambda nh, c: pl.BlockSpec((None, nh, rows, c), lambda bi, i: (bi, 0, i, 0))
    if is_prompt:
        kv_shape = jax.ShapeDtypeStruct((depth, b, r * N_KV_HEADS, LANES), _F32)
        kv_spec = pl.BlockSpec((None, None, rows * N_KV_HEADS, LANES), lambda bi, i: (layer, bi, i, 0))
        out_shape = [jax.ShapeDtypeStruct((b, N_STREAMS, r, LANES), _BF16), kv_shape, kv_shape]
        out_specs = [head_spec(N_STREAMS, LANES), kv_spec, kv_spec]
    else:
        out_shape = [jax.ShapeDtypeStruct((b, r, dq), _BF16), jax.ShapeDtypeStruct((b, r, dk), _F32),
                     jax.ShapeDtypeStruct((b, r, dv), _F32)]
        out_specs = [row_spec(dq), row_spec(dk), row_spec(dv)]
    out_shape += [jax.ShapeDtypeStruct((b, r, d), _F32), jax.ShapeDtypeStruct((b, r, d), _F32),
                  jax.ShapeDtypeStruct((b, hist, d), _F32), jax.ShapeDtypeStruct((b, stride, d), _F32)]
    out_specs += [row_spec(d), row_spec(d), state_spec(hist), state_spec(stride)]
    if is_prompt:
        out_shape += [jax.ShapeDtypeStruct((b, 2 * N_KV_HEADS, r, LANES), _BF16),
                      jax.ShapeDtypeStruct((b, N_KV_HEADS, r, 2 * D_VHEAD), _BF16)]
        out_specs += [head_spec(2 * N_KV_HEADS, LANES), head_spec(N_KV_HEADS, 2 * D_VHEAD)]
    kern = functools.partial(_pre_kernel, stride=stride, rows=rows, pad=pad, is_prompt=is_prompt,
                             n_alias=len(aliases))
    return pl.pallas_call(
        kern, grid=grid, in_specs=in_specs, out_specs=out_specs, out_shape=out_shape,
        input_output_aliases=aliases,
        scratch_shapes=[pltpu.VMEM((pad + rows, d), _F32), pltpu.VMEM((rows, d), _F32),
                        pltpu.VMEM((rows, d), _F32), pltpu.VMEM((rows, d), _F32),
                        pltpu.VMEM((stride, d), _F32)],
        compiler_params=pltpu.CompilerParams(dimension_semantics=("arbitrary", "arbitrary"),
                                             vmem_limit_bytes=VMEM_LIMIT_BYTES),
        name="pre_prompt" if is_prompt else "pre_sample",
    )(*args)


def _attn_prompt_kernel(qt_ref, kt_ref, q_ref, k_ref, v_ref, lq1_ref, lk1_ref, lq2_ref, lk2_ref, gs_ref,
                        o_ref, m_scr, acc_scr, *, lam_init, tq, tk):
    s_idx = pl.program_id(1)
    qi = qt_ref[s_idx]
    kj = kt_ref[s_idx]
    rows = GROUP * tq
    n_pairs = 2 * N_KV_HEADS

    n_sub = tk // tq
    is_last = kj == qi // n_sub

    @pl.when(jnp.logical_and(kj == 0, is_last))
    def _():
        m_scr[...] = jnp.full(m_scr.shape, NEG_BIG, _F32)
        acc_scr[...] = jnp.zeros(acc_scr.shape, _F32)

    def update(n_cols, masked, first=False):
        if masked:
            row = lax.broadcasted_iota(jnp.int32, (rows, tq), 0) % tq
            col = lax.broadcasted_iota(jnp.int32, (rows, tq), 1)
            keep = col <= row
        def scores(pair):
            q = q_ref[GROUP * pair:GROUP * (pair + 1)].reshape(rows, LANES)
            return _dot_nt(q, k_ref[pair, 0:n_cols, :])

        s_next = scores(0)
        for pair in range(n_pairs):
            s = s_next
            if pair + 1 < n_pairs:
                s_next = scores(pair + 1)
            if masked:
                diag = jnp.where(keep, s[:, n_cols - tq:], NEG_BIG)
                s = diag if n_cols == tq else jnp.concatenate([s[:, :n_cols - tq], diag], axis=1)
            if first:
                m_new = jnp.broadcast_to(jnp.max(s, axis=1, keepdims=True), (rows, LANES))
            else:
                m_prev = m_scr[pair]
                m_new = jnp.maximum(m_prev, jnp.max(s, axis=1, keepdims=True))
            p = jnp.exp2(s - jnp.concatenate([m_new] * (n_cols // LANES), axis=1))
            pv = _dot(p.astype(_BF16), v_ref[pair // 2, 0:n_cols, :])
            if first:
                acc_scr[pair] = pv
            else:
                alpha = jnp.exp2(m_prev - m_new)
                acc_scr[pair] = jnp.concatenate([alpha, alpha], axis=1) * acc_scr[pair] + pv
            m_scr[pair] = m_new

    @pl.when(jnp.logical_and(jnp.logical_not(is_last), kj > 0))
    def _():
        update(tk, False)

    @pl.when(jnp.logical_and(jnp.logical_not(is_last), kj == 0))
    def _():
        update(tk, False, first=True)

    for r in range(n_sub):
        @pl.when(jnp.logical_and(is_last, qi % n_sub == r))
        def _():
            update((r + 1) * tq, True)

    @pl.when(is_last)
    def _():
        lam = _lam_value(lq1_ref[...], lk1_ref[...], lq2_ref[...], lk2_ref[...], lam_init)
        for hg in range(N_HEADS):
            h, g = divmod(hg, GROUP)
            a0 = acc_scr[2 * h, g * tq:(g + 1) * tq, :]
            a1 = acc_scr[2 * h + 1, g * tq:(g + 1) * tq, :]
            o = a0[:, :D_VHEAD] / a0[:, D_VHEAD:] - lam * (a1[:, :D_VHEAD] / a1[:, D_VHEAD:])
            o_ref[:, hg * LANES:(hg + 1) * LANES] = (_rms(o, gs_ref[...]) * (1.0 - lam_init)).astype(o_ref.dtype)


def _attn_prompt_call(q, k, v, w, layer):
    b, _, t, _ = q.shape
    tq = min(ATTN_Q_BLOCK, t)
    tk = min(ATTN_K_BLOCK, t)
    assert t % tk == 0 and tk % tq == 0 and t <= POS_SPLIT * 256
    pairs = [(qi, kj) for qi in range(t // tq) for kj in range((qi * tq) // tk + 1)]
    qt = jnp.asarray([p[0] for p in pairs], jnp.int32)
    kt = jnp.asarray([p[1] for p in pairs], jnp.int32)
    small = lambda a: pl.BlockSpec((None,) + tuple(a.shape[1:]), lambda bi, s, qt, kt: (layer, 0, 0))
    grid_spec = pltpu.PrefetchScalarGridSpec(
        num_scalar_prefetch=2, grid=(b, len(pairs)),
        in_specs=[
            pl.BlockSpec((None, q.shape[1], tq, LANES), lambda bi, s, qt, kt: (bi, 0, qt[s], 0)),
            pl.BlockSpec((None, k.shape[1], tk, LANES), lambda bi, s, qt, kt: (bi, 0, kt[s], 0)),
            pl.BlockSpec((None, v.shape[1], tk, v.shape[-1]), lambda bi, s, qt, kt: (bi, 0, kt[s], 0)),
            small(w['lam_q1']), small(w['lam_k1']), small(w['lam_q2']), small(w['lam_k2']),
            small(w['g_subln']),
        ],
        out_specs=pl.BlockSpec((None, tq, N_HEADS * D_VHEAD), lambda bi, s, qt, kt: (bi, qt[s], 0)),
        scratch_shapes=[pltpu.VMEM((2 * N_KV_HEADS, GROUP * tq, LANES), _F32),
                        pltpu.VMEM((2 * N_KV_HEADS, GROUP * tq, 2 * D_VHEAD), _F32)],
    )
    kern = functools.partial(_attn_prompt_kernel, lam_init=_lam_init(layer), tq=tq, tk=tk)
    return pl.pallas_call(
        kern, grid_spec=grid_spec,
        out_shape=jax.ShapeDtypeStruct((b, t, N_HEADS * D_VHEAD), _BF16),
        compiler_params=pltpu.CompilerParams(dimension_semantics=("arbitrary", "arbitrary"),
                                             vmem_limit_bytes=VMEM_LIMIT_BYTES),
        name="attn_prompt",
    )(qt, kt, q, k, v, w['lam_q1'], w['lam_k1'], w['lam_q2'], w['lam_k2'], w['g_subln'])


def _attn_sample_kernel(pt_ref, q_ref, kn_ref, vn_ref, lq1_ref, lk1_ref, lq2_ref, lk2_ref, gs_ref, *rest,
                        lam_init, n_pages, page, n_new):
    k_pages = rest[:n_pages]
    v_pages = rest[n_pages:2 * n_pages]
    o_ref = rest[2 * n_pages]
    del pt_ref
    past = n_pages * page
    n_rows = q_ref.shape[0]
    rows_per_head = GROUP * 2 * n_new
    r_idx = lax.broadcasted_iota(jnp.int32, (n_rows, 1), 0)
    h_row = r_idx // rows_per_head
    hg_row = r_idx // (2 * n_new)
    t_row = r_idx % n_new
    slope = jnp.zeros((n_rows, 1), _F32)
    for hg in range(N_HEADS):
        slope = jnp.where(hg_row == hg, SLOPES[hg], slope)
    q_pos = (past + t_row).astype(_F32)

    width = page * N_KV_HEADS
    col = lax.broadcasted_iota(jnp.int32, (1, width), 1)
    valid = (col % N_KV_HEADS) == h_row
    base = jnp.where(valid, slope * ((col // N_KV_HEADS).astype(_F32) - q_pos), NEG_BIG)
    q = q_ref[...]
    s_pages = [_dot_nt(q, kp[...].astype(_BF16)) + (base + slope * float(j * page))
               for j, kp in enumerate(k_pages)]

    n_cols_new = kn_ref.shape[0]
    col_n = lax.broadcasted_iota(jnp.int32, (1, n_cols_new), 1)
    t_key = col_n // N_KV_HEADS
    valid_n = ((col_n % N_KV_HEADS) == h_row) & (t_key <= t_row)
    s_new = _dot_nt(q, kn_ref[...].astype(_BF16))
    s_new = jnp.where(valid_n, s_new - slope * (t_row - t_key).astype(_F32), NEG_BIG)

    m = jnp.max(s_new, axis=-1, keepdims=True)
    for s in s_pages:
        m = jnp.maximum(m, jnp.max(s, axis=-1, keepdims=True))
    p_new = jnp.exp(s_new - m)
    l = jnp.sum(p_new, axis=-1, keepdims=True)
    acc = _dot(p_new.astype(_BF16), vn_ref[...].astype(_BF16))
    for s, vp in zip(s_pages, v_pages):
        p = jnp.exp(s - m)
        l = l + jnp.sum(p, axis=-1, keepdims=True)
        acc = acc + _dot(p.astype(_BF16), vp[...].astype(_BF16))
    on = acc / l
    lam = _lam_value(lq1_ref[...], lk1_ref[...], lq2_ref[...], lk2_ref[...], lam_init)
    o = on - lam * pltpu.roll(on, n_rows - n_new, 0)
    o_ref[...] = (_rms(o, gs_ref[...]) * (1.0 - lam_init)).astype(o_ref.dtype)


def _attn_sample_call(q, k_new, v_new, cache_k, cache_v, page_table, w, layer):
    n, n_rows, _ = q.shape
    n_new = n_rows // (N_KV_HEADS * GROUP * 2)
    n_pages = page_table.shape[1]
    page = cache_k.shape[2] // N_KV_HEADS
    small = lambda a: pl.BlockSpec((None,) + tuple(a.shape[1:]), lambda bi, pt: (layer, 0, 0))
    per_seq = lambda a: pl.BlockSpec((None,) + tuple(a.shape[1:]), lambda bi, pt: (bi, 0, 0))
    page_spec = lambda j: pl.BlockSpec((None, None) + tuple(cache_k.shape[2:]),
                                       lambda bi, pt: (layer, pt[bi * n_pages + j], 0, 0))
    grid_spec = pltpu.PrefetchScalarGridSpec(
        num_scalar_prefetch=1, grid=(n,),
        in_specs=[per_seq(q), per_seq(k_new), per_seq(v_new),
                  small(w['lam_q1']), small(w['lam_k1']), small(w['lam_q2']), small(w['lam_k2']),
                  small(w['g_subln'])]
                 + [page_spec(j) for j in range(n_pages)] + [page_spec(j) for j in range(n_pages)],
        out_specs=pl.BlockSpec((None, n_rows, LANES), lambda bi, pt: (bi, 0, 0)),
    )
    kern = functools.partial(_attn_sample_kernel, lam_init=_lam_init(layer), n_pages=n_pages, page=page,
                             n_new=n_new)
    return pl.pallas_call(
        kern, grid_spec=grid_spec,
        out_shape=jax.ShapeDtypeStruct((n, n_rows, LANES), _BF16),
        compiler_params=pltpu.CompilerParams(dimension_semantics=("arbitrary",),
                                             vmem_limit_bytes=VMEM_LIMIT_BYTES),
        name="attn_sample",
    )(page_table.reshape(-1), q, k_new, v_new, w['lam_q1'], w['lam_k1'], w['lam_q2'], w['lam_k2'],
      w['g_subln'], *([cache_k] * n_pages), *([cache_v] * n_pages))


def _post_kernel(h_ref, m_ref, g_ref, o_ref, p_ref, wba_ref, wo_ref, gffn_ref, wg_ref, wu_ref, wd_ref,
                 gple_ref, wpg_ref, wp_ref, gfin_ref, out_ref, *, is_last):
    mix = m_ref[...] + g_ref[...] * _dot(o_ref[...], wba_ref[...])
    h = h_ref[...] + _dot(mix.astype(_BF16), wo_ref[...])
    n2 = _rms(h, gffn_ref[...]).astype(_BF16)
    act = jax.nn.silu(_dot(n2, wg_ref[...])) * _dot(n2, wu_ref[...])
    h = h + _dot(act.astype(_BF16), wd_ref[...])
    n3 = _rms(h, gple_ref[...]).astype(_BF16)
    pg = jax.nn.sigmoid(_dot(n3, wpg_ref[...]))
    h = h + _dot(p_ref[...].astype(_BF16), wp_ref[...]) * pg
    out_ref[...] = _rms(h, gfin_ref[...]) if is_last else h


def _post_call(h, m_lru, g_att, o, p, w, layer, *, is_last):
    r, d = h.shape
    rows = min(ROW_BLOCK, r)
    assert r % rows == 0
    row_spec = lambda c: pl.BlockSpec((rows, c), lambda i: (i, 0))
    names = ['w_br_att', 'w_o', 'g_ffn', 'w_gate', 'w_up', 'w_down', 'g_ple', 'w_ple_gate', 'w_ple']
    kern = functools.partial(_post_kernel, is_last=is_last)
    return pl.pallas_call(
        kern, grid=(r // rows,),
        in_specs=[row_spec(d), row_spec(d), row_spec(d), row_spec(o.shape[-1]),
                  pl.BlockSpec((None, rows, p.shape[-1]), lambda i: (layer, i, 0))]
                 + [_const_spec(w[nm].shape, layer) for nm in names]
                 + [pl.BlockSpec(w['g_final'].shape, lambda i: (0, 0), pipeline_mode=pl.Buffered(1))],
        out_specs=row_spec(d),
        out_shape=jax.ShapeDtypeStruct((r, d), _F32),
        compiler_params=pltpu.CompilerParams(dimension_semantics=("arbitrary",),
                                             vmem_limit_bytes=VMEM_LIMIT_BYTES),
        name="post",
    )(h, m_lru, g_att, o, p, *[w[nm] for nm in names], w['g_final'])


def kernel(x_prompt, x_sample, p_prompt, p_sample, cache_k, cache_v, page_table, state_conv, state_h,
           w_in, g_mix, conv_w, conv_b, w_a, b_a, w_i, b_i, lru_lambda, lam_q1, lam_k1, lam_q2, lam_k2,
           g_subln, w_br_lru, w_br_att, w_o, g_ffn, w_gate, w_up, w_down, g_ple, w_ple_gate, w_ple, g_final):
    depth, d = g_mix.shape
    bp, t, _ = x_prompt.shape
    ns, ts, _ = x_sample.shape
    hist = CONV_W - 1
    row3 = lambda a: a.reshape(depth, 1, a.shape[-1])
    w = dict(
        w_in=w_in.astype(_BF16), g_mix=row3(g_mix), conv_w=conv_w, conv_b=row3(conv_b),
        w_ai=jnp.concatenate([w_a, w_i], axis=-1).astype(_BF16), b_a=row3(b_a), b_i=row3(b_i),
        lru_lambda=row3(lru_lambda), w_br_lru=w_br_lru.astype(_BF16),
        lam_q1=row3(lam_q1), lam_k1=row3(lam_k1), lam_q2=row3(lam_q2), lam_k2=row3(lam_k2),
        g_subln=row3(g_subln), w_br_att=w_br_att.astype(_BF16), w_o=w_o.astype(_BF16), g_ffn=row3(g_ffn),
        w_gate=w_gate.astype(_BF16), w_up=w_up.astype(_BF16), w_down=w_down.astype(_BF16),
        g_ple=row3(g_ple), w_ple_gate=w_ple_gate.astype(_BF16), w_ple=w_ple.astype(_BF16),
        g_final=g_final.reshape(1, d),
    )
    ck = cache_k.reshape(cache_k.shape[0], cache_k.shape[1], -1, cache_k.shape[-1])
    cv = cache_v.reshape(cache_v.shape[0], cache_v.shape[1], -1, cache_v.shape[-1])
    dk = N_KV_HEADS * 2 * D_HEAD
    dv = N_KV_HEADS * D_VHEAD

    tm = lambda a: jnp.swapaxes(a, -3, -2)
    hp = x_prompt
    hs = tm(x_sample).reshape(1, ts * ns, d)
    cs_all = tm(state_conv).reshape(depth, 1, hist * ns, d)
    ps_all = tm(p_sample).reshape(depth, ts * ns, -1)
    zero_c = jnp.zeros((bp, hist, d), _F32)
    zero_h = jnp.zeros((bp, 1, d), _F32)
    rows_p = min(ROW_BLOCK, t)
    pp_all = p_prompt.reshape(depth, bp * t, -1)
    k_stack = jnp.zeros((depth, bp, t * N_KV_HEADS, LANES), _F32)
    v_stack = jnp.zeros((depth, bp, t * N_KV_HEADS, LANES), _F32)

    cs_p, hs_p, ks_s, vs_s, cs_s, hs_s = ([] for _ in range(6))
    for l in range(depth):
        last = l == depth - 1
        q, k_stack, v_stack, m_lru, g_att, c_out, h_out, k_rdy, v_aug = _pre_call(
            hp, zero_c, zero_h, w, l, stride=1, rows=rows_p, is_prompt=True,
            kv_stack=(k_stack, v_stack))
        o = _attn_prompt_call(q, k_rdy, v_aug, w, l)
        hp = _post_call(hp.reshape(bp * t, d), m_lru.reshape(bp * t, d), g_att.reshape(bp * t, d),
                        o.reshape(bp * t, -1), pp_all, w, l,
                        is_last=last).reshape(bp, t, d)
        cs_p.append(c_out)
        hs_p.append(h_out.reshape(bp, d))
        q, k, v, m_lru, g_att, c_out, h_out = _pre_call(
            hs, cs_all[l], state_h[l][None], w, l, stride=ns, rows=ts * ns, is_prompt=False)
        qs = q.reshape(ts, ns, N_KV_HEADS, GROUP, 2, LANES).transpose(1, 2, 3, 4, 0, 5)
        qs = qs.reshape(ns, N_STREAMS * ts, LANES)
        k_sm = tm(k.reshape(ts, ns, dk))
        v_sm = tm(v.reshape(ts, ns, dv))
        o = _attn_sample_call(qs, k_sm.reshape(ns, ts * N_KV_HEADS, -1), v_sm.reshape(ns, ts * N_KV_HEADS, -1),
                              ck, cv, page_table, w, l)
        o = o.reshape(ns, N_KV_HEADS, GROUP, 2, ts, D_VHEAD)[:, :, :, 0]
        o = o.transpose(3, 0, 1, 2, 4).reshape(ts * ns, -1)
        hs = _post_call(hs.reshape(ts * ns, d), m_lru.reshape(ts * ns, d), g_att.reshape(ts * ns, d),
                        o, ps_all, w, l, is_last=last).reshape(1, ts * ns, d)
        ks_s.append(k_sm.reshape(ns, ts, N_KV_HEADS, 2 * D_HEAD))
        vs_s.append(v_sm.reshape(ns, ts, N_KV_HEADS, D_VHEAD))
        cs_s.append(tm(c_out.reshape(hist, ns, d)))
        hs_s.append(h_out.reshape(ns, d))

    y_sample = tm(hs.reshape(ts, ns, d))
    k_prompt = k_stack.reshape(depth, bp, t, N_KV_HEADS, 2 * D_HEAD)
    v_prompt = v_stack.reshape(depth, bp, t, N_KV_HEADS, D_VHEAD)
    return (hp, y_sample, k_prompt, v_prompt, jnp.stack(cs_p), jnp.stack(hs_p),
            jnp.stack(ks_s), jnp.stack(vs_s), jnp.stack(cs_s), jnp.stack(hs_s))
```
